```python
import math
import jax, jax.numpy as jnp
from jax import lax
import numpy as np

D_MODEL = 1024
BATCH = 4
SEQ = 4096
DEPTH = 1
DEC_BATCH = 16
DEC_SEQ = 2048
PAST_LEN = 128

HEAD_DIM = 64
DIFF_HEADS = 4
DIFF_QK_DIM = HEAD_DIM
DIFF_V_DIM = 2 * HEAD_DIM
GQA_HEADS = 8
GQA_KV_HEADS = 2
GQA_GROUP = GQA_HEADS // GQA_KV_HEADS
DIFF_WIDTH = DIFF_HEADS * DIFF_V_DIM
GQA_WIDTH = GQA_HEADS * HEAD_DIM
MIX_WIDTH = DIFF_WIDTH + GQA_WIDTH
COL_SIZES = (DIFF_HEADS * 2 * DIFF_QK_DIM,
             DIFF_HEADS * 2 * DIFF_QK_DIM,
             DIFF_HEADS * DIFF_V_DIM,
             GQA_HEADS * HEAD_DIM,
             GQA_KV_HEADS * HEAD_DIM,
             GQA_KV_HEADS * HEAD_DIM)
IN_COLS = sum(COL_SIZES)
SPLITS = [int(v) for v in np.cumsum(COL_SIZES)[:-1]]
ROPE_THETA = 500000.0
PARTIAL_ROPE_DIM = HEAD_DIM // 4
AXIAL_THETA = 10000.0
GRID_W = 64
Q_BLOCK = 128
N_MEM = 256
XATTN_HEADS = 4
XATTN_HEAD_DIM = D_MODEL // XATTN_HEADS
D_FF = 2816
CONV_WIDTH = 3
EPS = 1e-6

kernel_name = "hybrid_diff_gqa2d_memxattn_convffn_encoder"


def _rmsnorm(x, g):
    x32 = x.astype(jnp.float32)
    y = x32 * lax.rsqrt(jnp.mean(x32 * x32, axis=-1, keepdims=True) + EPS)
    return (y * g.astype(jnp.float32)).astype(x.dtype)


def _rope_cos_sin(pos, dim, theta):
    inv = theta ** (-jnp.arange(0, dim, 2, dtype=jnp.float32) / dim)
    ang = pos.astype(jnp.float32)[:, None] * inv[None, :]
    return jnp.cos(ang), jnp.sin(ang)


def _rotate(x, cos, sin):
    half = x.shape[-1] // 2
    x1 = x[..., :half].astype(jnp.float32)
    x2 = x[..., half:].astype(jnp.float32)
    return jnp.concatenate([x1 * cos - x2 * sin, x1 * sin + x2 * cos], axis=-1).astype(x.dtype)


def _partial_rope(x, cos, sin):
    return jnp.concatenate([_rotate(x[..., :PARTIAL_ROPE_DIM], cos, sin), x[..., PARTIAL_ROPE_DIM:]], axis=-1)


def _axial_rope(x, cos_r, sin_r, cos_c, sin_c):
    h = HEAD_DIM // 2
    return jnp.concatenate([_rotate(x[..., :h], cos_r, sin_r), _rotate(x[..., h:], cos_c, sin_c)], axis=-1)


def _sweep_query_blocks(fn, q):
    S = q.shape[-2]
    nblk = S // Q_BLOCK
    qb = jnp.moveaxis(q.reshape(q.shape[:-2] + (nblk, Q_BLOCK, q.shape[-1])), -3, 0)
    out = lax.map(fn, qb)
    out = jnp.moveaxis(out, 0, -3)
    return out.reshape(out.shape[:-3] + (S, out.shape[-1]))


def _diff_attention(q, k, v, lam):
    scale = DIFF_QK_DIM ** -0.5
    kf = k.astype(jnp.float32)

    def block(qb):
        s = jnp.einsum('bhmqd,bhmkd->bhmqk', qb.astype(jnp.float32), kf) * scale
        p = jax.nn.softmax(s, axis=-1)
        a = p[:, :, 0] - lam * p[:, :, 1]
        return jnp.einsum('bhqk,bhkd->bhqd', a.astype(v.dtype), v)

    return _sweep_query_blocks(block, q)


def _gqa_attention(q, k, v):
    scale = HEAD_DIM ** -0.5
    kf = k.astype(jnp.float32)

    def block(qb):
        s = jnp.einsum('bkgqd,bksd->bkgqs', qb.astype(jnp.float32), kf) * scale
        p = jax.nn.softmax(s, axis=-1)
        return jnp.einsum('bkgqs,bksd->bkgqd', p.astype(v.dtype), v)

    return _sweep_query_blocks(block, q)


def _encoder_layer(x, mem, rope_tabs, layer_idx, norm_mix_g, w_in, lambda_q1, lambda_k1, lambda_q2, lambda_k2,
                   diff_subln_g, gqa_q_norm_g, gqa_k_norm_g, w_out, norm_xattn_g, norm_mem_g, w_xq, w_xkv, w_xo,
                   norm_ffn_g, w_up, conv_w, conv_b, w_down):
    B, S, _ = x.shape
    cos_p, sin_p, cos_r, sin_r, cos_c, sin_c = rope_tabs

    h = _rmsnorm(x, norm_mix_g)
    proj = jnp.einsum('bsd,de->bse', h, w_in)
    dq, dk, dv, gq, gk, gv = jnp.split(proj, SPLITS, axis=-1)

    dq = _partial_rope(dq.reshape(B, S, DIFF_HEADS, 2, DIFF_QK_DIM).transpose(0, 2, 3, 1, 4), cos_p, sin_p)
    dk = _partial_rope(dk.reshape(B, S, DIFF_HEADS, 2, DIFF_QK_DIM).transpose(0, 2, 3, 1, 4), cos_p, sin_p)
    dv = dv.reshape(B, S, DIFF_HEADS, DIFF_V_DIM).transpose(0, 2, 1, 3)
    lambda_init = 0.8 - 0.6 * math.exp(-0.3 * layer_idx)
    lam = (jnp.exp(jnp.sum(lambda_q1.astype(jnp.float32) * lambda_k1.astype(jnp.float32)))
           - jnp.exp(jnp.sum(lambda_q2.astype(jnp.float32) * lambda_k2.astype(jnp.float32)))
           + lambda_init)
    d_out = _diff_attention(dq, dk, dv, lam)
    d_out = _rmsnorm(d_out, diff_subln_g) * (1.0 - lambda_init)
    d_out = d_out.transpose(0, 2, 1, 3).reshape(B, S, DIFF_WIDTH)

    gq = _rmsnorm(gq.reshape(B, S, GQA_HEADS, HEAD_DIM), gqa_q_norm_g)
    gq = gq.reshape(B, S, GQA_KV_HEADS, GQA_GROUP, HEAD_DIM).transpose(0, 2, 3, 1, 4)
    gq = _axial_rope(gq, cos_r, sin_r, cos_c, sin_c)
    gk = _rmsnorm(gk.reshape(B, S, GQA_KV_HEADS, HEAD_DIM), gqa_k_norm_g).transpose(0, 2, 1, 3)
    gk = _axial_rope(gk, cos_r, sin_r, cos_c, sin_c)
    gv = gv.reshape(B, S, GQA_KV_HEADS, HEAD_DIM).transpose(0, 2, 1, 3)
    g_out = _gqa_attention(gq, gk, gv)
    g_out = g_out.transpose(0, 3, 1, 2, 4).reshape(B, S, GQA_WIDTH)

    x = x + jnp.einsum('bse,ed->bsd', jnp.concatenate([d_out, g_out], axis=-1), w_out)

    h = _rmsnorm(x, norm_xattn_g)
    m = _rmsnorm(mem, norm_mem_g)
    n_mem = mem.shape[1]
    q = jnp.einsum('bsd,de->bse', h, w_xq).reshape(B, S, XATTN_HEADS, XATTN_HEAD_DIM)
    kv = jnp.einsum('bmd,de->bme', m, w_xkv).reshape(B, n_mem, 2, XATTN_HEADS, XATTN_HEAD_DIM)
    mk, mv = kv[:, :, 0], kv[:, :, 1]
    s = jnp.einsum('bshd,bmhd->bhsm', q.astype(jnp.float32), mk.astype(jnp.float32)) * (XATTN_HEAD_DIM ** -0.5)
    p = jax.nn.softmax(s, axis=-1)
    o = jnp.einsum('bhsm,bmhd->bshd', p.astype(mv.dtype), mv).reshape(B, S, D_MODEL)
    x = x + jnp.einsum('bsd,de->bse', o, w_xo)

    h = _rmsnorm(x, norm_ffn_g)
    u = jnp.einsum('bsd,df->bsf', h, w_up)
    up = jnp.pad(u, ((0, 0), (1, 1), (0, 0)))
    u = up[:, :-2] * conv_w[0] + up[:, 1:-1] * conv_w[1] + up[:, 2:] * conv_w[2] + conv_b
    a, b = jnp.split(u, 2, axis=-1)
    x = x + jnp.einsum('bsf,fd->bsd', jax.nn.silu(a) * b, w_down)
    return x


def _run_trunk(x, mem, norm_mix_g, w_in, lambda_q1, lambda_k1, lambda_q2, lambda_k2, diff_subln_g,
               gqa_q_norm_g, gqa_k_norm_g, w_out, norm_xattn_g, norm_mem_g, w_xq, w_xkv, w_xo,
               norm_ffn_g, w_up, conv_w, conv_b, w_down, final_norm_g):
    S = x.shape[1]
    rows = S // GRID_W
    pos = jnp.arange(S, dtype=jnp.int32)
    row_idx = jnp.repeat(jnp.arange(rows, dtype=jnp.int32), GRID_W)
    col_idx = jnp.tile(jnp.arange(GRID_W, dtype=jnp.int32), rows)
    cos_p, sin_p = _rope_cos_sin(pos, PARTIAL_ROPE_DIM, ROPE_THETA)
    cos_r, sin_r = _rope_cos_sin(row_idx, HEAD_DIM // 2, AXIAL_THETA)
    cos_c, sin_c = _rope_cos_sin(col_idx, HEAD_DIM // 2, AXIAL_THETA)
    tabs = (cos_p, sin_p, cos_r, sin_r, cos_c, sin_c)
    for l in range(DEPTH):
        x = _encoder_layer(x, mem, tabs, l, norm_mix_g[l], w_in[l], lambda_q1[l], lambda_k1[l], lambda_q2[l],
                           lambda_k2[l], diff_subln_g[l], gqa_q_norm_g[l], gqa_k_norm_g[l], w_out[l],
                           norm_xattn_g[l], norm_mem_g[l], w_xq[l], w_xkv[l], w_xo[l], norm_ffn_g[l],
                           w_up[l], conv_w[l], conv_b[l], w_down[l])
    return _rmsnorm(x, final_norm_g)


def setup_inputs(seed: int = 0) -> dict:
    key = jax.random.key(seed)
    ks = jax.random.split(key, 26)
    f32 = jnp.float32

    def nrm(k, shape, scale):
        return jax.random.normal(k, shape, f32) * scale

    def gain(k, shape):
        return 1.0 + 0.02 * jax.random.normal(k, shape, f32)

    L = DEPTH
    return {
        "x_prompt": nrm(ks[0], (BATCH, SEQ, D_MODEL), 1.0),
        "x_sample": nrm(ks[1], (DEC_BATCH, DEC_SEQ, D_MODEL), 1.0),
        "mem_prompt": nrm(ks[2], (BATCH, N_MEM, D_MODEL), 1.0),
        "mem_sample": nrm(ks[3], (DEC_BATCH, N_MEM, D_MODEL), 1.0),
        "norm_mix_g": gain(ks[4], (L, D_MODEL)),
        "w_in": nrm(ks[5], (L, D_MODEL, IN_COLS), D_MODEL ** -0.5),
        "lambda_q1": nrm(ks[6], (L, DIFF_QK_DIM), 0.1),
        "lambda_k1": nrm(ks[7], (L, DIFF_QK_DIM), 0.1),
        "lambda_q2": nrm(ks[8], (L, DIFF_QK_DIM), 0.1),
        "lambda_k2": nrm(ks[9], (L, DIFF_QK_DIM), 0.1),
        "diff_subln_g": gain(ks[10], (L, DIFF_V_DIM)),
        "gqa_q_norm_g": gain(ks[11], (L, HEAD_DIM)),
        "gqa_k_norm_g": gain(ks[12], (L, HEAD_DIM)),
        "w_out": nrm(ks[13], (L, MIX_WIDTH, D_MODEL), MIX_WIDTH ** -0.5),
        "norm_xattn_g": gain(ks[14], (L, D_MODEL)),
        "norm_mem_g": gain(ks[15], (L, D_MODEL)),
        "w_xq": nrm(ks[16], (L, D_MODEL, D_MODEL), D_MODEL ** -0.5),
        "w_xkv": nrm(ks[17], (L, D_MODEL, 2 * D_MODEL), D_MODEL ** -0.5),
        "w_xo": nrm(ks[18], (L, D_MODEL, D_MODEL), D_MODEL ** -0.5),
        "norm_ffn_g": gain(ks[19], (L, D_MODEL)),
        "w_up": nrm(ks[20], (L, D_MODEL, 2 * D_FF), D_MODEL ** -0.5),
        "conv_w": nrm(ks[21], (L, CONV_WIDTH, 2 * D_FF), 0.5),
        "conv_b": nrm(ks[22], (L, 2 * D_FF), 0.02),
        "w_down": nrm(ks[23], (L, D_FF, D_MODEL), D_FF ** -0.5),
        "final_norm_g": gain(ks[24], (D_MODEL,)),
    }


def reference(x_prompt, x_sample, mem_prompt, mem_sample, norm_mix_g, w_in, lambda_q1, lambda_k1, lambda_q2,
              lambda_k2, diff_subln_g, gqa_q_norm_g, gqa_k_norm_g, w_out, norm_xattn_g, norm_mem_g, w_xq, w_xkv,
              w_xo, norm_ffn_g, w_up, conv_w, conv_b, w_down, final_norm_g):
    y_prompt = _run_trunk(x_prompt, mem_prompt, norm_mix_g, w_in, lambda_q1, lambda_k1, lambda_q2, lambda_k2,
                          diff_subln_g, gqa_q_norm_g, gqa_k_norm_g, w_out, norm_xattn_g, norm_mem_g, w_xq, w_xkv,
                          w_xo, norm_ffn_g, w_up, conv_w, conv_b, w_down, final_norm_g)
    y_sample = _run_trunk(x_sample, mem_sample, norm_mix_g, w_in, lambda_q1, lambda_k1, lambda_q2, lambda_k2,
                          diff_subln_g, gqa_q_norm_g, gqa_k_norm_g, w_out, norm_xattn_g, norm_mem_g, w_xq, w_xkv,
                          w_xo, norm_ffn_g, w_up, conv_w, conv_b, w_down, final_norm_g)
    return (y_prompt, y_sample)
```

```python
import functools
import math

import jax
import jax.numpy as jnp
from jax import lax
from jax.experimental import pallas as pl
from jax.experimental.pallas import tpu as pltpu

F32 = jnp.float32
BF16 = jnp.bfloat16

D_MODEL = 1024
HEAD_DIM = 64
DIFF_HEADS = 4
GQA_HEADS = 8
GQA_KV_HEADS = 2
DIFF_W = 512
GQA_W = 512
KV_W = GQA_KV_HEADS * HEAD_DIM
IN_COLS = 2304
ROPE_THETA = 500000.0
PARTIAL_ROPE_DIM = 16
AXIAL_THETA = 10000.0
GRID_W = 64
N_MEM = 256
XATTN_HEADS = 4
XATTN_HEAD_DIM = 256
D_FF = 2816
EPS = 1e-6
LAMBDA_INIT = 0.8 - 0.6 * math.exp(-0.3 * 0)
LOG2E = 1.4426950408889634

LANES = 128
HALO = 8
TOKEN_TILE = 512
FF_CHUNK = 256
ATTN_SCORE_ELEMS = 512 * 4096
VMEM_LIMIT = 56 * 1024 * 1024


def _rms(x, g):
    ms = jnp.mean(x * x, axis=-1, keepdims=True)
    return x * lax.rsqrt(ms + EPS) * g


def _rope(x, c, s_up, s_dn, shift):
    return x * c + pltpu.roll(x, LANES - shift, 1) * s_up + pltpu.roll(x, shift, 1) * s_dn


def _head_pair_rms(x, g, lo):
    sq = x * x
    tot = jnp.sum(sq, axis=-1, keepdims=True)
    first = jnp.sum(jnp.where(lo, sq, 0.0), axis=-1, keepdims=True)
    ss = jnp.where(lo, first, tot - first)
    return x * lax.rsqrt(ss * (1.0 / HEAD_DIM) + EPS) * g


def _proj_kernel(x_ref, g_ref, w_ref, cp_ref, sup_ref, sdp_ref, ca_ref, sua_ref, sda_ref, gqn_ref, gkn_ref,
                 dq_ref, dk_ref, dv_ref, gq_ref, gk_ref, gv_ref):
    tm = x_ref.shape[0]
    h = _rms(x_ref[...], g_ref[...]).astype(BF16)
    lo = lax.broadcasted_iota(jnp.int32, (tm, LANES), 1) < HEAD_DIM
    cp, sup, sdp = cp_ref[...], sup_ref[...], sdp_ref[...]
    ca, sua, sda = ca_ref[...], sua_ref[...], sda_ref[...]
    qscale = HEAD_DIM ** -0.5 * LOG2E

    def cols(start, width):
        return jnp.dot(h, w_ref[:, start:start + width], preferred_element_type=F32)

    for c in range(DIFF_W // LANES):
        q = cols(c * LANES, LANES)
        dq_ref[:, c * LANES:(c + 1) * LANES] = (_rope(q, cp, sup, sdp, 8) * qscale).astype(BF16)
        k = cols(DIFF_W + c * LANES, LANES)
        dk_ref[:, c * LANES:(c + 1) * LANES] = _rope(k, cp, sup, sdp, 8).astype(BF16)
    dv_ref[...] = cols(2 * DIFF_W, DIFF_W).astype(BF16)

    gq0 = 3 * DIFF_W
    gqn = gqn_ref[...]
    for c in range(GQA_W // LANES):
        q = _head_pair_rms(cols(gq0 + c * LANES, LANES), gqn, lo)
        gq_ref[:, c * LANES:(c + 1) * LANES] = (_rope(q, ca, sua, sda, 16) * qscale).astype(BF16)

    k = _head_pair_rms(cols(gq0 + GQA_W, KV_W), gkn_ref[...], lo)
    k = _rope(k, ca, sua, sda, 16)
    v = cols(gq0 + GQA_W + KV_W, KV_W)
    for src, dst in ((k, gk_ref), (v, gv_ref)):
        swapped = pltpu.roll(src, HEAD_DIM, 1)
        dst[:, 0:LANES] = jnp.where(lo, src, swapped).astype(BF16)
        dst[:, LANES:2 * LANES] = jnp.where(lo, swapped, src).astype(BF16)


def _proj(x, P, S):
    T = x.shape[0]
    tm = TOKEN_TILE
    tiles_per_seq = S // tm
    row = lambda i: (i, 0)
    const = lambda i: (0, 0)
    tab = lambda i: (i % tiles_per_seq, 0)
    tab_spec = pl.BlockSpec((tm, LANES), tab)
    vec = lambda n: pl.BlockSpec((1, n), const)
    out_w = (DIFF_W, DIFF_W, DIFF_W, GQA_W, 2 * LANES, 2 * LANES)
    return pl.pallas_call(
        _proj_kernel,
        grid=(T // tm,),
        in_specs=[pl.BlockSpec((tm, D_MODEL), row), vec(D_MODEL),
                  pl.BlockSpec((D_MODEL, IN_COLS), const)] + [tab_spec] * 6 + [vec(LANES), vec(LANES)],
        out_specs=[pl.BlockSpec((tm, w), row) for w in out_w],
        out_shape=[jax.ShapeDtypeStruct((T, w), BF16) for w in out_w],
        compiler_params=pltpu.CompilerParams(dimension_semantics=("arbitrary",), vmem_limit_bytes=VMEM_LIMIT),
        name="proj",
    )(x, P["norm_mix_g"], P["w_in"], *P["tabs"], P["gqa_q_norm_g"], P["gqa_k_norm_g"])


def _stack_queries(q_ref, qs_ref, tq, nstack):
    lo = lax.broadcasted_iota(jnp.int32, (tq, LANES), 1) < HEAD_DIM
    for j in range(nstack):
        blk = q_ref[:, (j // 2) * LANES:(j // 2 + 1) * LANES]
        keep = lo if j % 2 == 0 else jnp.logical_not(lo)
        qs_ref[j * tq:(j + 1) * tq, :] = jnp.where(keep, blk, jnp.zeros_like(blk))
    return lo


def _softmax_pv(qs_ref, k_ref, v_ref):
    s = lax.dot_general(qs_ref[...], k_ref[...], (((1,), (1,)), ((), ())), preferred_element_type=F32)
    m = jnp.max(s, axis=-1, keepdims=True)
    p = jnp.exp2(s - m)
    l = jnp.sum(p, axis=-1, keepdims=True)
    o = jnp.dot(p.astype(BF16), v_ref[...], preferred_element_type=F32)
    return o / l


def _diff_attn_kernel(q_ref, k_ref, v_ref, lq1_ref, lk1_ref, lq2_ref, lk2_ref, sg_ref, o_ref, qs_ref):
    tq = q_ref.shape[0]
    _stack_queries(q_ref, qs_ref, tq, 2)
    o = _softmax_pv(qs_ref, k_ref, v_ref)
    lam = (jnp.exp(jnp.sum(lq1_ref[...] * lk1_ref[...], axis=-1, keepdims=True))
           - jnp.exp(jnp.sum(lq2_ref[...] * lk2_ref[...], axis=-1, keepdims=True)) + LAMBDA_INIT)
    d = o[0:tq] - lam * o[tq:2 * tq]
    o_ref[...] = (_rms(d, sg_ref[...]) * (1.0 - LAMBDA_INIT)).astype(BF16)


def _gqa_attn_kernel(q_ref, k_ref, v_ref, o_ref, qs_ref):
    tq = q_ref.shape[0]
    group = GQA_HEADS // GQA_KV_HEADS
    lo = _stack_queries(q_ref, qs_ref, tq, group)
    o = _softmax_pv(qs_ref, k_ref, v_ref)
    for p in range(group // 2):
        even = o[(2 * p) * tq:(2 * p + 1) * tq]
        odd = o[(2 * p + 1) * tq:(2 * p + 2) * tq]
        o_ref[:, p * LANES:(p + 1) * LANES] = jnp.where(lo, even, odd).astype(BF16)


def _diff_attn(dq, dk, dv, P, B, S):
    T = dq.shape[0]
    tq = ATTN_SCORE_ELEMS // S // 2
    nq = S // tq
    qmap = lambda b, h, i: (b * nq + i, h)
    kvmap = lambda b, h, i: (b, h)
    vec = lambda n: pl.BlockSpec((1, n), lambda b, h, i: (0, 0))
    return pl.pallas_call(
        _diff_attn_kernel,
        grid=(B, DIFF_HEADS, nq),
        in_specs=[pl.BlockSpec((tq, LANES), qmap), pl.BlockSpec((S, LANES), kvmap),
                  pl.BlockSpec((S, LANES), kvmap)] + [vec(HEAD_DIM)] * 4 + [vec(LANES)],
        out_specs=pl.BlockSpec((tq, LANES), qmap),
        out_shape=jax.ShapeDtypeStruct((T, DIFF_W), BF16),
        scratch_shapes=[pltpu.VMEM((2 * tq, LANES), BF16)],
        compiler_params=pltpu.CompilerParams(dimension_semantics=("arbitrary",) * 3, vmem_limit_bytes=VMEM_LIMIT),
        name="diff_attn",
    )(dq, dk, dv, P["lambda_q1"], P["lambda_k1"], P["lambda_q2"], P["lambda_k2"], P["diff_subln_g"])


def _gqa_attn(gq, gk, gv, B, S):
    T = gq.shape[0]
    group = GQA_HEADS // GQA_KV_HEADS
    tq = ATTN_SCORE_ELEMS // S // group
    nq = S // tq
    qmap = lambda b, g, i: (b * nq + i, g)
    kvmap = lambda b, g, i: (b, g)
    return pl.pallas_call(
        _gqa_attn_kernel,
        grid=(B, GQA_KV_HEADS, nq),
        in_specs=[pl.BlockSpec((tq, 2 * LANES), qmap), pl.BlockSpec((S, LANES), kvmap),
                  pl.BlockSpec((S, LANES), kvmap)],
        out_specs=pl.BlockSpec((tq, 2 * LANES), qmap),
        out_shape=jax.ShapeDtypeStruct((T, GQA_W), BF16),
        scratch_shapes=[pltpu.VMEM((group * tq, LANES), BF16)],
        compiler_params=pltpu.CompilerParams(dimension_semantics=("arbitrary",) * 3, vmem_limit_bytes=VMEM_LIMIT),
        name="gqa_attn",
    )(gq, gk, gv)


def _memkv_kernel(m_ref, g_ref, w_ref, o_ref):
    m = _rms(m_ref[...], g_ref[...]).astype(BF16)
    o_ref[...] = jnp.dot(m, w_ref[...], preferred_element_type=F32).astype(BF16)


def _memkv(mem, P):
    rows = mem.shape[0]
    return pl.pallas_call(
        _memkv_kernel,
        grid=(rows // N_MEM,),
        in_specs=[pl.BlockSpec((N_MEM, D_MODEL), lambda b: (b, 0)), pl.BlockSpec((1, D_MODEL), lambda b: (0, 0)),
                  pl.BlockSpec((D_MODEL, 2 * D_MODEL), lambda b: (0, 0))],
        out_specs=pl.BlockSpec((N_MEM, 2 * D_MODEL), lambda b: (b, 0)),
        out_shape=jax.ShapeDtypeStruct((rows, 2 * D_MODEL), BF16),
        compiler_params=pltpu.CompilerParams(dimension_semantics=("arbitrary",), vmem_limit_bytes=VMEM_LIMIT),
        name="memkv",
    )(mem, P["norm_mem_g"], P["w_xkv"])


def _post_kernel(x_ref, d_ref, g_ref, mk_ref, mv_ref, wout_ref, gx_ref, wxq_ref, wxo_ref, o_ref):
    x1 = (x_ref[...]
          + jnp.dot(d_ref[...], wout_ref[0:DIFF_W, :], preferred_element_type=F32)
          + jnp.dot(g_ref[...], wout_ref[DIFF_W:DIFF_W + GQA_W, :], preferred_element_type=F32))
    h = _rms(x1, gx_ref[...]).astype(BF16)
    q = (jnp.dot(h, wxq_ref[...], preferred_element_type=F32) * (XATTN_HEAD_DIM ** -0.5 * LOG2E)).astype(BF16)
    outs = []
    for hh in range(XATTN_HEADS):
        sl = slice(hh * XATTN_HEAD_DIM, (hh + 1) * XATTN_HEAD_DIM)
        s = lax.dot_general(q[:, sl], mk_ref[:, sl], (((1,), (1,)), ((), ())), preferred_element_type=F32)
        m = jnp.max(s, axis=-1, keepdims=True)
        p = jnp.exp2(s - m)
        l = jnp.sum(p, axis=-1, keepdims=True)
        o = jnp.dot(p.astype(BF16), mv_ref[:, sl], preferred_element_type=F32) / l
        outs.append(o.astype(BF16))
    o = jnp.concatenate(outs, axis=-1)
    o_ref[...] = x1 + jnp.dot(o, wxo_ref[...], preferred_element_type=F32)


def _post(x, d_out, g_out, kv, P, S):
    T = x.shape[0]
    tm = TOKEN_TILE
    tiles_per_seq = S // tm
    row = lambda i: (i, 0)
    const = lambda i: (0, 0)
    return pl.pallas_call(
        _post_kernel,
        grid=(T // tm,),
        in_specs=[pl.BlockSpec((tm, D_MODEL), row), pl.BlockSpec((tm, DIFF_W), row), pl.BlockSpec((tm, GQA_W), row),
                  pl.BlockSpec((N_MEM, D_MODEL), lambda i: (i // tiles_per_seq, 0)),
                  pl.BlockSpec((N_MEM, D_MODEL), lambda i: (i // tiles_per_seq, 1)),
                  pl.BlockSpec((D_MODEL, D_MODEL), const), pl.BlockSpec((1, D_MODEL), const),
                  pl.BlockSpec((D_MODEL, D_MODEL), const), pl.BlockSpec((D_MODEL, D_MODEL), const)],
        out_specs=pl.BlockSpec((tm, D_MODEL), row),
        out_shape=jax.ShapeDtypeStruct((T, D_MODEL), F32),
        compiler_params=pltpu.CompilerParams(dimension_semantics=("arbitrary",), vmem_limit_bytes=VMEM_LIMIT),
        name="post",
    )(x, d_out, g_out, kv, kv, P["w_out"], P["norm_xattn_g"], P["w_xq"], P["w_xo"])


def _ffn_kernel(x_ref, prev_ref, next_ref, gn_ref, wup_ref, cw_ref, cb_ref, wdn_ref, gf_ref, o_ref, act_ref,
                *, tiles_per_seq):
    tm = x_ref.shape[0]
    j = pl.program_id(0) % tiles_per_seq
    has_prev = j > 0
    has_next = j < tiles_per_seq - 1
    x = x_ref[...]
    gn = gn_ref[...]
    h = _rms(x, gn).astype(BF16)
    h_halo = _rms(jnp.concatenate([prev_ref[...], next_ref[...]], axis=0), gn).astype(BF16)
    rid = lax.broadcasted_iota(jnp.int32, (tm, FF_CHUNK), 0)
    is_first = rid == 0
    is_last = rid == tm - 1

    def conv_cols(start):
        w = wup_ref[:, start:start + FF_CHUNK]
        u = jnp.dot(h, w, preferred_element_type=F32)
        uh = jnp.dot(h_halo, w, preferred_element_type=F32)
        prev_row = jnp.where(has_prev, uh[HALO - 1:HALO, :], 0.0)
        next_row = jnp.where(has_next, uh[HALO:HALO + 1, :], 0.0)
        up = jnp.where(is_first, prev_row, pltpu.roll(u, 1, 0))
        un = jnp.where(is_last, next_row, pltpu.roll(u, tm - 1, 0))
        cw = cw_ref[:, start:start + FF_CHUNK]
        return up * cw[0:1, :] + u * cw[1:2, :] + un * cw[2:3, :] + cb_ref[:, start:start + FF_CHUNK]

    for c in range(D_FF // FF_CHUNK):
        a = conv_cols(c * FF_CHUNK)
        b = conv_cols(D_FF + c * FF_CHUNK)
        act_ref[:, c * FF_CHUNK:(c + 1) * FF_CHUNK] = (a / (1.0 + jnp.exp(-a)) * b).astype(BF16)

    x3 = x + jnp.dot(act_ref[...], wdn_ref[...], preferred_element_type=F32)
    o_ref[...] = _rms(x3, gf_ref[...])


def _ffn(x, P, S):
    T = x.shape[0]
    tm = TOKEN_TILE
    halo_per_tile = tm // HALO
    last_halo = T // HALO - 1
    row = lambda i: (i, 0)
    const = lambda i: (0, 0)
    return pl.pallas_call(
        functools.partial(_ffn_kernel, tiles_per_seq=S // tm),
        grid=(T // tm,),
        in_specs=[pl.BlockSpec((tm, D_MODEL), row),
                  pl.BlockSpec((HALO, D_MODEL), lambda i: (jnp.maximum(i * halo_per_tile - 1, 0), 0)),
                  pl.BlockSpec((HALO, D_MODEL), lambda i: (jnp.minimum((i + 1) * halo_per_tile, last_halo), 0)),
                  pl.BlockSpec((1, D_MODEL), const),
                  pl.BlockSpec((D_MODEL, 2 * D_FF), const, pipeline_mode=pl.Buffered(1)),
                  pl.BlockSpec((3, 2 * D_FF), const), pl.BlockSpec((1, 2 * D_FF), const),
                  pl.BlockSpec((D_FF, D_MODEL), const, pipeline_mode=pl.Buffered(1)),
                  pl.BlockSpec((1, D_MODEL), const)],
        out_specs=pl.BlockSpec((tm, D_MODEL), row),
        out_shape=jax.ShapeDtypeStruct((T, D_MODEL), F32),
        scratch_shapes=[pltpu.VMEM((tm, D_FF), BF16)],
        compiler_params=pltpu.CompilerParams(dimension_semantics=("arbitrary",), vmem_limit_bytes=VMEM_LIMIT),
        name="ffn",
    )(x, x, x, P["norm_ffn_g"], P["w_up"], P["conv_w"], P["conv_b"], P["w_down"], P["final_norm_g"])


def _rope_tables(S):
    pos = jnp.arange(S, dtype=jnp.int32)
    lane = jnp.arange(LANES, dtype=jnp.int32) % HEAD_DIM

    def angles(p, dim, theta):
        inv = theta ** (-jnp.arange(0, dim, 2, dtype=F32) / dim)
        return p.astype(F32)[:, None] * inv[None, :]

    half = PARTIAL_ROPE_DIM // 2
    ang = angles(pos, PARTIAL_ROPE_DIM, ROPE_THETA)[:, lane % half]
    rot = (lane < PARTIAL_ROPE_DIM)[None, :]
    first = ((lane % PARTIAL_ROPE_DIM) < half)[None, :]
    cp = jnp.where(rot, jnp.cos(ang), 1.0)
    sup = jnp.where(rot & first, -jnp.sin(ang), 0.0)
    sdp = jnp.where(rot & ~first, jnp.sin(ang), 0.0)

    half = HEAD_DIM // 4
    ang_r = angles(pos // GRID_W, HEAD_DIM // 2, AXIAL_THETA)[:, lane % half]
    ang_c = angles(pos % GRID_W, HEAD_DIM // 2, AXIAL_THETA)[:, lane % half]
    ang = jnp.where((lane < HEAD_DIM // 2)[None, :], ang_r, ang_c)
    first = ((lane % (HEAD_DIM // 2)) < half)[None, :]
    ca = jnp.cos(ang)
    sua = jnp.where(first, -jnp.sin(ang), 0.0)
    sda = jnp.where(first, 0.0, jnp.sin(ang))
    return tuple(t.astype(F32) for t in (cp, sup, sdp, ca, sua, sda))


def _trunk(x, mem, P):
    B, S, _ = x.shape
    xf = x.reshape(B * S, D_MODEL)
    dq, dk, dv, gq, gk, gv = _proj(xf, P, S)
    d_out = _diff_attn(dq, dk, dv, P, B, S)
    g_out = _gqa_attn(gq, gk, gv, B, S)
    kv = _memkv(mem.reshape(B * N_MEM, D_MODEL), P)
    x2 = _post(xf, d_out, g_out, kv, P, S)
    return _ffn(x2, P, S).reshape(B, S, D_MODEL)


def kernel(x_prompt, x_sample, mem_prompt, mem_sample, norm_mix_g, w_in, lambda_q1, lambda_k1, lambda_q2, lambda_k2, diff_subln_g, gqa_q_norm_g, gqa_k_norm_g, w_out, norm_xattn_g, norm_mem_g, w_xq, w_xkv, w_xo, norm_ffn_g, w_up, conv_w, conv_b, w_down, final_norm_g):
    assert w_in.shape[0] == 1, "single-layer trunk"
    tile2 = lambda g: jnp.concatenate([g, g], axis=-1)
    P = {
        "norm_mix_g": norm_mix_g[0][None, :], "w_in": w_in[0].astype(BF16),
        "lambda_q1": lambda_q1, "lambda_k1": lambda_k1, "lambda_q2": lambda_q2, "lambda_k2": lambda_k2,
        "diff_subln_g": diff_subln_g, "gqa_q_norm_g": tile2(gqa_q_norm_g), "gqa_k_norm_g": tile2(gqa_k_norm_g),
        "w_out": w_out[0].astype(BF16), "norm_xattn_g": norm_xattn_g, "norm_mem_g": norm_mem_g,
        "w_xq": w_xq[0].astype(BF16), "w_xkv": w_xkv[0].astype(BF16), "w_xo": w_xo[0].astype(BF16),
        "norm_ffn_g": norm_ffn_g, "w_up": w_up[0].astype(BF16), "conv_w": conv_w[0], "conv_b": conv_b,
        "w_down": w_down[0].astype(BF16), "final_norm_g": final_norm_g[None, :],
        "tabs": _rope_tables(max(x_prompt.shape[1], x_sample.shape[1])),
    }
    y_prompt = _trunk(x_prompt, mem_prompt, P)
    y_sample = _trunk(x_sample, mem_sample, P)
    return (y_prompt, y_sample)
```

```python
import functools
import math

import jax
import jax.numpy as jnp
from jax import lax
from jax.experimental import pallas as pl
from jax.experimental.pallas import tpu as pltpu

F32 = jnp.float32
BF16 = jnp.bfloat16

D_MODEL = 1024
HEAD_DIM = 64
DIFF_HEADS = 4
GQA_HEADS = 8
GQA_KV_HEADS = 2
DIFF_W = 512
GQA_W = 512
KV_W = GQA_KV_HEADS * HEAD_DIM
ROPE_THETA = 500000.0
PARTIAL_ROPE_DIM = 16
AXIAL_THETA = 10000.0
GRID_W = 64
N_MEM = 256
XATTN_HEADS = 4
XATTN_HEAD_DIM = 256
D_FF = 2816
EPS = 1e-6
LAMBDA_INIT = 0.8 - 0.6 * math.exp(-0.3 * 0)
LOG2E = 1.4426950408889634

LANES = 128
SUBLANES = 8
HALO = SUBLANES
TOKEN_TILE = 512
FF_CHUNK = 256
ATTN_UNIT_ELEMS = 4096 * 256
ATTN_ROW_CHUNK = 512
N_PAIRS = 4
VMEM_LIMIT = 56 * 1024 * 1024

NT_DIMS = (((1,), (1,)), ((), ()))


def _rms(x, g):
    ms = jnp.mean(x * x, axis=-1, keepdims=True)
    return x * lax.rsqrt(ms + EPS) * g


def _rope(x, c, s_up, s_dn, shift):
    return x * c + pltpu.roll(x, LANES - shift, 1) * s_up + pltpu.roll(x, shift, 1) * s_dn


def _head_pair_rms(x, g, lo):
    sq = x * x
    tot = jnp.sum(sq, axis=-1, keepdims=True)
    first = jnp.sum(jnp.where(lo, sq, 0.0), axis=-1, keepdims=True)
    ss = jnp.where(lo, first, tot - first)
    return x * lax.rsqrt(ss * (1.0 / HEAD_DIM) + EPS) * g


def _rotate_rows(a, b, cos, sin):
    return a * cos - b * sin, a * sin + b * cos


def _proj_kernel(x_ref, g_ref, wk_ref, wt_ref, cp_ref, sup_ref, sdp_ref, ca_ref, sua_ref, sda_ref, gkn_ref,
                 cpt_ref, spt_ref, crt_ref, srt_ref, cct_ref, sct_ref, gqn_ref,
                 dqt_ref, dk_ref, dvt_ref, gqt_ref, gk_ref, gvt_ref):
    tm = x_ref.shape[0]
    h = _rms(x_ref[...], g_ref[...]).astype(BF16)
    qscale = HEAD_DIM ** -0.5 * LOG2E

    def rows_t(start, n):
        return lax.dot_general(wt_ref[start:start + n, :], h, NT_DIMS, preferred_element_type=F32)

    cpt, spt = cpt_ref[...], spt_ref[...]
    half = PARTIAL_ROPE_DIM // 2
    for c in range(DIFF_HEADS):
        q = rows_t(c * LANES, LANES)
        parts = []
        for m in range(2):
            b0 = m * HEAD_DIM
            ra, rb = _rotate_rows(q[b0:b0 + half], q[b0 + half:b0 + 2 * half], cpt, spt)
            parts += [ra, rb, q[b0 + 2 * half:b0 + HEAD_DIM]]
        dqt_ref[c] = (jnp.concatenate(parts, axis=0) * qscale).astype(BF16)
        dvt_ref[c] = rows_t(DIFF_W + c * LANES, LANES).astype(BF16)

    crt, srt, cct, sct = crt_ref[...], srt_ref[...], cct_ref[...], sct_ref[...]
    gqn = gqn_ref[...]
    quarter = HEAD_DIM // 4
    for c in range(GQA_HEADS // 2):
        q = rows_t(2 * DIFF_W + c * LANES, LANES)
        parts = []
        for m in range(2):
            x = q[m * HEAD_DIM:(m + 1) * HEAD_DIM]
            ss = jnp.sum(x * x, axis=0, keepdims=True)
            x = x * lax.rsqrt(ss * (1.0 / HEAD_DIM) + EPS) * gqn
            parts += _rotate_rows(x[0:quarter], x[quarter:2 * quarter], crt, srt)
            parts += _rotate_rows(x[2 * quarter:3 * quarter], x[3 * quarter:HEAD_DIM], cct, sct)
        gqt_ref[c] = (jnp.concatenate(parts, axis=0) * qscale).astype(BF16)
    gvt_ref[...] = rows_t(2 * DIFF_W + GQA_W, KV_W).astype(BF16)

    lo = lax.broadcasted_iota(jnp.int32, (tm, LANES), 1) < HEAD_DIM
    cp, sup, sdp = cp_ref[...], sup_ref[...], sdp_ref[...]
    for c in range(DIFF_HEADS):
        k = jnp.dot(h, wk_ref[:, c * LANES:(c + 1) * LANES], preferred_element_type=F32)
        dk_ref[c] = _rope(k, cp, sup, sdp, half).astype(BF16)
    k = jnp.dot(h, wk_ref[:, DIFF_W:DIFF_W + KV_W], preferred_element_type=F32)
    k = _head_pair_rms(k, gkn_ref[...], lo)
    gk_ref[...] = _rope(k, ca_ref[...], sua_ref[...], sda_ref[...], quarter).astype(BF16)


def _proj(x, P, S):
    T = x.shape[0]
    tm = TOKEN_TILE
    tiles_per_seq = S // tm
    row = lambda i: (i, 0)
    const = lambda i: (0, 0)
    tab_spec = pl.BlockSpec((tm, LANES), lambda i: (i % tiles_per_seq, 0))
    tab_t = lambda n: pl.BlockSpec((n, tm), lambda i: (0, i % tiles_per_seq))
    vec = lambda n: pl.BlockSpec((1, n), const)
    head_t = lambda n, r: pl.BlockSpec((n, r, tm), lambda i: (0, 0, i))
    wk, wt = P["w_in_k"], P["w_in_t"]
    return pl.pallas_call(
        _proj_kernel,
        grid=(T // tm,),
        in_specs=[pl.BlockSpec((tm, D_MODEL), row), vec(D_MODEL),
                  pl.BlockSpec(wk.shape, const), pl.BlockSpec(wt.shape, const)]
                 + [tab_spec] * 6 + [vec(LANES)]
                 + [tab_t(8), tab_t(8), tab_t(16), tab_t(16), tab_t(16), tab_t(16)]
                 + [pl.BlockSpec((HEAD_DIM, tm), const)],
        out_specs=[head_t(DIFF_HEADS, LANES), pl.BlockSpec((DIFF_HEADS, tm, LANES), lambda i: (0, i, 0)),
                   head_t(DIFF_HEADS, LANES), head_t(GQA_HEADS // 2, LANES),
                   pl.BlockSpec((tm, LANES), row), pl.BlockSpec((KV_W, tm), lambda i: (0, i))],
        out_shape=[jax.ShapeDtypeStruct((DIFF_HEADS, LANES, T), BF16),
                   jax.ShapeDtypeStruct((DIFF_HEADS, T, LANES), BF16),
                   jax.ShapeDtypeStruct((DIFF_HEADS, LANES, T), BF16),
                   jax.ShapeDtypeStruct((GQA_HEADS // 2, LANES, T), BF16),
                   jax.ShapeDtypeStruct((T, LANES), BF16),
                   jax.ShapeDtypeStruct((KV_W, T), BF16)],
        compiler_params=pltpu.CompilerParams(dimension_semantics=("arbitrary",), vmem_limit_bytes=VMEM_LIMIT),
        name="proj",
    )(x, P["norm_mix_g"], wk, wt, *P["tabs"], P["gqa_k_norm_g"], *P["tabs_t"], P["gqa_q_norm_g_t"])


def _score_unit(q_rows, hi, k, s_ref, m_ref):
    z = jnp.zeros_like(q_rows)
    lo_pad = jnp.concatenate([q_rows, z], axis=0)
    hi_pad = jnp.concatenate([z, q_rows], axis=0)
    if isinstance(hi, bool):
        qpad = hi_pad if hi else lo_pad
    else:
        qpad = jnp.where(hi, hi_pad, lo_pad)
    s = jnp.dot(k, qpad, preferred_element_type=F32)
    s_ref[...] = s
    n, qb = s.shape
    parts = [jnp.max(s[c:c + ATTN_ROW_CHUNK].reshape(ATTN_ROW_CHUNK // SUBLANES, SUBLANES, qb), axis=0)
             for c in range(0, n, ATTN_ROW_CHUNK)]
    while len(parts) > 1:
        parts = [jnp.maximum(a, b) for a, b in zip(parts[0::2], parts[1::2])]
    m_ref[...] = jnp.max(parts[0], axis=0, keepdims=True)


def _softmax_pv(s_ref, m_ref, vt, pt_ref):
    n, qb = s_ref.shape
    m = m_ref[...]
    parts = []
    for c in range(0, n, ATTN_ROW_CHUNK):
        p = jnp.exp2(s_ref[c:c + ATTN_ROW_CHUNK, :] - m)
        parts.append(jnp.sum(p.reshape(ATTN_ROW_CHUNK // SUBLANES, SUBLANES, qb), axis=0))
        pt_ref[c:c + ATTN_ROW_CHUNK, :] = p.astype(BF16)
    while len(parts) > 1:
        parts = [a + b for a, b in zip(parts[0::2], parts[1::2])]
    l = jnp.sum(parts[0], axis=0, keepdims=True)
    o = jnp.dot(vt, pt_ref[...], preferred_element_type=F32)
    return o / l


def _attn_kernel(*refs, diff):
    if diff:
        (qt_ref, qtn_ref, k_ref, kn_ref, vt_ref, lq1_ref, lk1_ref, lq2_ref, lk2_ref, sg_ref,
         o_ref, s0, s1, m0, m1, pt0, pt1) = refs
    else:
        qt_ref, qtn_ref, k_ref, kn_ref, vt_ref, o_ref, s0, s1, m0, m1, pt0, pt1 = refs
    first_step = jnp.logical_and(pl.program_id(0) == 0, pl.program_id(1) == 0)

    def keys(j):
        return k_ref[j] if diff else k_ref[...]

    def values(j):
        return vt_ref[j] if diff else vt_ref[j // 2]

    def second_hi(j):
        return True if diff else (j // 2) == 1

    def first_hi(j):
        return False if diff else (j // 2) == 1

    @pl.when(first_step)
    def _():
        _score_unit(qt_ref[0, 0:HEAD_DIM, :], False, keys(0), s0, m0)

    def pair(j, next_q, next_hi, next_k):
        _score_unit(qt_ref[j, HEAD_DIM:2 * HEAD_DIM, :], second_hi(j), keys(j), s1, m1)
        oa = _softmax_pv(s0, m0, values(j), pt0)
        _score_unit(next_q, next_hi, next_k, s0, m0)
        ob = _softmax_pv(s1, m1, values(j), pt1)
        if diff:
            lam = (jnp.exp(jnp.sum(lq1_ref[...] * lk1_ref[...], axis=-1, keepdims=True))
                   - jnp.exp(jnp.sum(lq2_ref[...] * lk2_ref[...], axis=-1, keepdims=True)) + LAMBDA_INIT)
            d = oa - lam * ob
            ms = jnp.mean(d * d, axis=0, keepdims=True)
            out = d * lax.rsqrt(ms + EPS) * sg_ref[...] * (1.0 - LAMBDA_INIT)
        else:
            out = jnp.concatenate([oa, ob], axis=0)
        o_ref[j] = out.T.astype(BF16)

    def body(j, carry):
        pair(j, qt_ref[j + 1, 0:HEAD_DIM, :], first_hi(j + 1), keys(j + 1))
        return carry

    lax.fori_loop(0, N_PAIRS - 1, body, 0)
    kn = kn_ref[0] if diff else kn_ref[...]
    pair(N_PAIRS - 1, qtn_ref[0, 0:HEAD_DIM, :], False, kn)


def _attn(qt, k, vt, P, B, S, diff):
    T = qt.shape[-1]
    qb = ATTN_UNIT_ELEMS // S
    nq = S // qb
    last = B * nq - 1
    cur = lambda b, i: b * nq + i
    nxt = lambda b, i: jnp.minimum(b * nq + i + 1, last)
    const = lambda b, i: (0, 0)
    if diff:
        k_specs = [pl.BlockSpec((DIFF_HEADS, S, LANES), lambda b, i: (0, b, 0)),
                   pl.BlockSpec((1, S, LANES), lambda b, i: (0, nxt(b, i) // nq, 0))]
        vt_spec = pl.BlockSpec((DIFF_HEADS, LANES, S), lambda b, i: (0, 0, b))
        extra = [P["lambda_q1"], P["lambda_k1"], P["lambda_q2"], P["lambda_k2"], P["diff_subln_g_t"]]
        extra_specs = [pl.BlockSpec((1, HEAD_DIM), const)] * 4 + [pl.BlockSpec((LANES, qb), const)]
        name = "diff_attn"
    else:
        k_specs = [pl.BlockSpec((S, LANES), lambda b, i: (b, 0)),
                   pl.BlockSpec((S, LANES), lambda b, i: (nxt(b, i) // nq, 0))]
        vt = vt.reshape(GQA_KV_HEADS, HEAD_DIM, T)
        vt_spec = pl.BlockSpec((GQA_KV_HEADS, HEAD_DIM, S), lambda b, i: (0, 0, b))
        extra, extra_specs = [], []
        name = "gqa_attn"
    return pl.pallas_call(
        functools.partial(_attn_kernel, diff=diff),
        grid=(B, nq),
        in_specs=[pl.BlockSpec((N_PAIRS, LANES, qb), lambda b, i: (0, 0, cur(b, i))),
                  pl.BlockSpec((1, LANES, qb), lambda b, i: (0, 0, nxt(b, i)))] + k_specs + [vt_spec] + extra_specs,
        out_specs=pl.BlockSpec((N_PAIRS, qb, LANES), lambda b, i: (0, cur(b, i), 0)),
        out_shape=jax.ShapeDtypeStruct((N_PAIRS, T, LANES), BF16),
        scratch_shapes=[pltpu.VMEM((S, qb), F32), pltpu.VMEM((S, qb), F32),
                        pltpu.VMEM((1, qb), F32), pltpu.VMEM((1, qb), F32),
                        pltpu.VMEM((S, qb), BF16), pltpu.VMEM((S, qb), BF16)],
        compiler_params=pltpu.CompilerParams(dimension_semantics=("arbitrary", "arbitrary"),
                                             vmem_limit_bytes=VMEM_LIMIT),
        name=name,
    )(qt, qt, k, k, vt, *extra)


def _memkv_kernel(m_ref, g_ref, w_ref, o_ref):
    m = _rms(m_ref[...], g_ref[...]).astype(BF16)
    o_ref[...] = jnp.dot(m, w_ref[...], preferred_element_type=F32).astype(BF16)


def _memkv(mem, P):
    rows = mem.shape[0]
    return pl.pallas_call(
        _memkv_kernel,
        grid=(rows // N_MEM,),
        in_specs=[pl.BlockSpec((N_MEM, D_MODEL), lambda b: (b, 0)), pl.BlockSpec((1, D_MODEL), lambda b: (0, 0)),
                  pl.BlockSpec((D_MODEL, 2 * D_MODEL), lambda b: (0, 0))],
        out_specs=pl.BlockSpec((N_MEM, 2 * D_MODEL), lambda b: (b, 0)),
        out_shape=jax.ShapeDtypeStruct((rows, 2 * D_MODEL), BF16),
        compiler_params=pltpu.CompilerParams(dimension_semantics=("arbitrary",), vmem_limit_bytes=VMEM_LIMIT),
        name="memkv",
    )(mem, P["norm_mem_g"], P["w_xkv"])


def _post_kernel(x_ref, d_ref, g_ref, mk_ref, mv_ref, wout_ref, gx_ref, wxq_ref, wxo_ref, o_ref):
    mix = jnp.concatenate([d_ref[c] for c in range(N_PAIRS)] + [g_ref[c] for c in range(N_PAIRS)], axis=-1)
    x1 = x_ref[...] + jnp.dot(mix, wout_ref[...], preferred_element_type=F32)
    h = _rms(x1, gx_ref[...]).astype(BF16)
    q = (jnp.dot(h, wxq_ref[...], preferred_element_type=F32) * (XATTN_HEAD_DIM ** -0.5 * LOG2E)).astype(BF16)
    outs = []
    for hh in range(XATTN_HEADS):
        sl = slice(hh * XATTN_HEAD_DIM, (hh + 1) * XATTN_HEAD_DIM)
        s = lax.dot_general(q[:, sl], mk_ref[:, sl], NT_DIMS, preferred_element_type=F32)
        m = jnp.max(s, axis=-1, keepdims=True)
        p = jnp.exp2(s - m)
        l = jnp.sum(p, axis=-1, keepdims=True)
        o = jnp.dot(p.astype(BF16), mv_ref[:, sl], preferred_element_type=F32) / l
        outs.append(o.astype(BF16))
    o = jnp.concatenate(outs, axis=-1)
    o_ref[...] = x1 + jnp.dot(o, wxo_ref[...], preferred_element_type=F32)


def _post(x, d_out, g_out, kv, P, S):
    T = x.shape[0]
    tm = TOKEN_TILE
    tiles_per_seq = S // tm
    row = lambda i: (i, 0)
    const = lambda i: (0, 0)
    heads = pl.BlockSpec((N_PAIRS, tm, LANES), lambda i: (0, i, 0))
    return pl.pallas_call(
        _post_kernel,
        grid=(T // tm,),
        in_specs=[pl.BlockSpec((tm, D_MODEL), row), heads, heads,
                  pl.BlockSpec((N_MEM, D_MODEL), lambda i: (i // tiles_per_seq, 0)),
                  pl.BlockSpec((N_MEM, D_MODEL), lambda i: (i // tiles_per_seq, 1)),
                  pl.BlockSpec((D_MODEL, D_MODEL), const), pl.BlockSpec((1, D_MODEL), const),
                  pl.BlockSpec((D_MODEL, D_MODEL), const), pl.BlockSpec((D_MODEL, D_MODEL), const)],
        out_specs=pl.BlockSpec((tm, D_MODEL), row),
        out_shape=jax.ShapeDtypeStruct((T, D_MODEL), F32),
        compiler_params=pltpu.CompilerParams(dimension_semantics=("arbitrary",), vmem_limit_bytes=VMEM_LIMIT),
        name="post",
    )(x, d_out, g_out, kv, kv, P["w_out"], P["norm_xattn_g"], P["w_xq"], P["w_xo"])


def _ffn_kernel(x_ref, prev_ref, next_ref, gn_ref, wup_ref, cw_ref, cb_ref, wdn_ref, gf_ref, o_ref, act_ref,
                *, tiles_per_seq):
    tm = x_ref.shape[0]
    j = pl.program_id(0) % tiles_per_seq
    has_prev = j > 0
    has_next = j < tiles_per_seq - 1
    x = x_ref[...]
    gn = gn_ref[...]
    h = _rms(x, gn).astype(BF16)
    h_halo = _rms(jnp.concatenate([prev_ref[...], next_ref[...]], axis=0), gn).astype(BF16)
    rid = lax.broadcasted_iota(jnp.int32, (tm, FF_CHUNK), 0)
    is_first = rid == 0
    is_last = rid == tm - 1

    def conv_cols(start):
        w = wup_ref[:, start:start + FF_CHUNK]
        u = jnp.dot(h, w, preferred_element_type=F32)
        uh = jnp.dot(h_halo, w, preferred_element_type=F32)
        prev_row = jnp.where(has_prev, uh[HALO - 1:HALO, :], 0.0)
        next_row = jnp.where(has_next, uh[HALO:HALO + 1, :], 0.0)
        up = jnp.where(is_first, prev_row, pltpu.roll(u, 1, 0))
        un = jnp.where(is_last, next_row, pltpu.roll(u, tm - 1, 0))
        cw = cw_ref[:, start:start + FF_CHUNK]
        return up * cw[0:1, :] + u * cw[1:2, :] + un * cw[2:3, :] + cb_ref[:, start:start + FF_CHUNK]

    for c in range(D_FF // FF_CHUNK):
        a = conv_cols(c * FF_CHUNK)
        b = conv_cols(D_FF + c * FF_CHUNK)
        act_ref[:, c * FF_CHUNK:(c + 1) * FF_CHUNK] = (a / (1.0 + jnp.exp(-a)) * b).astype(BF16)

    x3 = x + jnp.dot(act_ref[...], wdn_ref[...], preferred_element_type=F32)
    o_ref[...] = _rms(x3, gf_ref[...])


def _ffn(x, P, S):
    T = x.shape[0]
    tm = TOKEN_TILE
    halo_per_tile = tm // HALO
    last_halo = T // HALO - 1
    row = lambda i: (i, 0)
    const = lambda i: (0, 0)
    return pl.pallas_call(
        functools.partial(_ffn_kernel, tiles_per_seq=S // tm),
        grid=(T // tm,),
        in_specs=[pl.BlockSpec((tm, D_MODEL), row),
                  pl.BlockSpec((HALO, D_MODEL), lambda i: (jnp.maximum(i * halo_per_tile - 1, 0), 0)),
                  pl.BlockSpec((HALO, D_MODEL), lambda i: (jnp.minimum((i + 1) * halo_per_tile, last_halo), 0)),
                  pl.BlockSpec((1, D_MODEL), const),
                  pl.BlockSpec((D_MODEL, 2 * D_FF), const, pipeline_mode=pl.Buffered(1)),
                  pl.BlockSpec((3, 2 * D_FF), const), pl.BlockSpec((1, 2 * D_FF), const),
                  pl.BlockSpec((D_FF, D_MODEL), const, pipeline_mode=pl.Buffered(1)),
                  pl.BlockSpec((1, D_MODEL), const)],
        out_specs=pl.BlockSpec((tm, D_MODEL), row),
        out_shape=jax.ShapeDtypeStruct((T, D_MODEL), F32),
        scratch_shapes=[pltpu.VMEM((tm, D_FF), BF16)],
        compiler_params=pltpu.CompilerParams(dimension_semantics=("arbitrary",), vmem_limit_bytes=VMEM_LIMIT),
        name="ffn",
    )(x, x, x, P["norm_ffn_g"], P["w_up"], P["conv_w"], P["conv_b"], P["w_down"], P["final_norm_g"])


def _angles(p, dim, theta):
    inv = theta ** (-jnp.arange(0, dim, 2, dtype=F32) / dim)
    return p.astype(F32)[:, None] * inv[None, :]


def _rope_tables(S):
    pos = jnp.arange(S, dtype=jnp.int32)
    lane = jnp.arange(LANES, dtype=jnp.int32) % HEAD_DIM

    half = PARTIAL_ROPE_DIM // 2
    ang = _angles(pos, PARTIAL_ROPE_DIM, ROPE_THETA)[:, lane % half]
    rot = (lane < PARTIAL_ROPE_DIM)[None, :]
    first = ((lane % PARTIAL_ROPE_DIM) < half)[None, :]
    cp = jnp.where(rot, jnp.cos(ang), 1.0)
    sup = jnp.where(rot & first, -jnp.sin(ang), 0.0)
    sdp = jnp.where(rot & ~first, jnp.sin(ang), 0.0)

    half = HEAD_DIM // 4
    ang_r = _angles(pos // GRID_W, HEAD_DIM // 2, AXIAL_THETA)[:, lane % half]
    ang_c = _angles(pos % GRID_W, HEAD_DIM // 2, AXIAL_THETA)[:, lane % half]
    ang = jnp.where((lane < HEAD_DIM // 2)[None, :], ang_r, ang_c)
    first = ((lane % (HEAD_DIM // 2)) < half)[None, :]
    ca = jnp.cos(ang)
    sua = jnp.where(first, -jnp.sin(ang), 0.0)
    sda = jnp.where(first, 0.0, jnp.sin(ang))
    return tuple(t.astype(F32) for t in (cp, sup, sdp, ca, sua, sda))


def _rope_tables_t(S):
    pos = jnp.arange(S, dtype=jnp.int32)
    ang_p = _angles(pos, PARTIAL_ROPE_DIM, ROPE_THETA).T
    ang_r = _angles(pos // GRID_W, HEAD_DIM // 2, AXIAL_THETA).T
    ang_c = _angles(pos % GRID_W, HEAD_DIM // 2, AXIAL_THETA).T
    return (jnp.cos(ang_p), jnp.sin(ang_p), jnp.cos(ang_r), jnp.sin(ang_r), jnp.cos(ang_c), jnp.sin(ang_c))


def _trunk(x, mem, P):
    B, S, _ = x.shape
    xf = x.reshape(B * S, D_MODEL)
    dqt, dk, dvt, gqt, gk, gvt = _proj(xf, P, S)
    d_out = _attn(dqt, dk, dvt, P, B, S, diff=True)
    g_out = _attn(gqt, gk, gvt, P, B, S, diff=False)
    kv = _memkv(mem.reshape(B * N_MEM, D_MODEL), P)
    x2 = _post(xf, d_out, g_out, kv, P, S)
    return _ffn(x2, P, S).reshape(B, S, D_MODEL)


def kernel(x_prompt, x_sample, mem_prompt, mem_sample, norm_mix_g, w_in, lambda_q1, lambda_k1, lambda_q2, lambda_k2, diff_subln_g, gqa_q_norm_g, gqa_k_norm_g, w_out, norm_xattn_g, norm_mem_g, w_xq, w_xkv, w_xo, norm_ffn_g, w_up, conv_w, conv_b, w_down, final_norm_g):
    assert w_in.shape[0] == 1, "single-layer trunk"
    s_max = max(x_prompt.shape[1], x_sample.shape[1])
    qb_max = max(ATTN_UNIT_ELEMS // x_prompt.shape[1], ATTN_UNIT_ELEMS // x_sample.shape[1])
    w = w_in[0].astype(BF16)
    q0, k0, v0, gq0, gk0, gv0, end = 0, 512, 1024, 1536, 2048, 2176, 2304
    w_in_k = jnp.concatenate([w[:, k0:v0], w[:, gk0:gv0]], axis=1)
    w_in_t = jnp.concatenate([w[:, q0:k0], w[:, v0:gq0], w[:, gq0:gk0], w[:, gv0:end]], axis=1).T
    P = {
        "norm_mix_g": norm_mix_g[0][None, :], "w_in_k": w_in_k, "w_in_t": w_in_t,
        "lambda_q1": lambda_q1, "lambda_k1": lambda_k1, "lambda_q2": lambda_q2, "lambda_k2": lambda_k2,
        "diff_subln_g_t": jnp.broadcast_to(diff_subln_g[0][:, None], (LANES, qb_max)),
        "gqa_q_norm_g_t": jnp.broadcast_to(gqa_q_norm_g[0][:, None], (HEAD_DIM, TOKEN_TILE)),
        "gqa_k_norm_g": jnp.concatenate([gqa_k_norm_g, gqa_k_norm_g], axis=-1),
        "w_out": w_out[0].astype(BF16), "norm_xattn_g": norm_xattn_g, "norm_mem_g": norm_mem_g,
        "w_xq": w_xq[0].astype(BF16), "w_xkv": w_xkv[0].astype(BF16), "w_xo": w_xo[0].astype(BF16),
        "norm_ffn_g": norm_ffn_g, "w_up": w_up[0].astype(BF16), "conv_w": conv_w[0], "conv_b": conv_b,
        "w_down": w_down[0].astype(BF16), "final_norm_g": final_norm_g[None, :],
        "tabs": _rope_tables(s_max), "tabs_t": _rope_tables_t(s_max),
    }
    y_prompt = _trunk(x_prompt, mem_prompt, P)
    y_sample = _trunk(x_sample, mem_sample, P)
    return (y_prompt, y_sample)
```

```python
import functools
import math

import jax
import jax.numpy as jnp
from jax import lax
from jax.experimental import pallas as pl
from jax.experimental.pallas import tpu as pltpu

F32 = jnp.float32
BF16 = jnp.bfloat16

D_MODEL = 1024
HEAD_DIM = 64
DIFF_HEADS = 4
GQA_HEADS = 8
GQA_KV_HEADS = 2
DIFF_W = 512
GQA_W = 512
KV_W = GQA_KV_HEADS * HEAD_DIM
ROPE_THETA = 500000.0
PARTIAL_ROPE_DIM = 16
AXIAL_THETA = 10000.0
GRID_W = 64
N_MEM = 256
XATTN_HEADS = 4
XATTN_HEAD_DIM = 256
D_FF = 2816
EPS = 1e-6
LAMBDA_INIT = 0.8 - 0.6 * math.exp(-0.3 * 0)
LOG2E = 1.4426950408889634

LANES = 128
SUBLANES = 8
HALO = SUBLANES
TOKEN_TILE = 512
FF_CHUNK = 256
ATTN_UNIT_ELEMS = 4096 * 256
ATTN_SCORE_CHUNK = 1024
ATTN_ROW_CHUNK = 512
N_PAIRS = 4
VMEM_LIMIT = 56 * 1024 * 1024

NT_DIMS = (((1,), (1,)), ((), ()))


def _rms(x, g):
    ms = jnp.mean(x * x, axis=-1, keepdims=True)
    return x * lax.rsqrt(ms + EPS) * g


def _rope(x, c, s_up, s_dn, shift):
    return x * c + pltpu.roll(x, LANES - shift, 1) * s_up + pltpu.roll(x, shift, 1) * s_dn


def _head_pair_rms(x, g, lo):
    sq = x * x
    tot = jnp.sum(sq, axis=-1, keepdims=True)
    first = jnp.sum(jnp.where(lo, sq, 0.0), axis=-1, keepdims=True)
    ss = jnp.where(lo, first, tot - first)
    return x * lax.rsqrt(ss * (1.0 / HEAD_DIM) + EPS) * g


def _rotate_rows(a, b, cos, sin):
    return a * cos - b * sin, a * sin + b * cos


def _proj_kernel(x_ref, g_ref, wk_ref, wt_ref, cp_ref, sup_ref, sdp_ref, ca_ref, sua_ref, sda_ref, gkn_ref,
                 cpt_ref, spt_ref, crt_ref, srt_ref, cct_ref, sct_ref, gqn_ref,
                 dqt_ref, dk_ref, dvt_ref, gqt_ref, gk_ref, gvt_ref):
    tm = x_ref.shape[0]
    h = _rms(x_ref[...], g_ref[...]).astype(BF16)
    qscale = HEAD_DIM ** -0.5 * LOG2E

    proj_t = lax.dot_general(wt_ref[...], h, NT_DIMS, preferred_element_type=F32)
    proj_k = jnp.dot(h, wk_ref[...], preferred_element_type=F32)

    def rows_t(start, n):
        return proj_t[start:start + n]

    cpt, spt = cpt_ref[...], spt_ref[...]
    half = PARTIAL_ROPE_DIM // 2
    for c in range(DIFF_HEADS):
        q = rows_t(c * LANES, LANES)
        parts = []
        for m in range(2):
            b0 = m * HEAD_DIM
            ra, rb = _rotate_rows(q[b0:b0 + half], q[b0 + half:b0 + 2 * half], cpt, spt)
            parts += [ra, rb, q[b0 + 2 * half:b0 + HEAD_DIM]]
        dqt_ref[c] = (jnp.concatenate(parts, axis=0) * qscale).astype(BF16)
        dvt_ref[c] = rows_t(DIFF_W + c * LANES, LANES).astype(BF16)

    crt, srt, cct, sct = crt_ref[...], srt_ref[...], cct_ref[...], sct_ref[...]
    gqn = gqn_ref[...]
    quarter = HEAD_DIM // 4
    for c in range(GQA_HEADS // 2):
        q = rows_t(2 * DIFF_W + c * LANES, LANES)
        parts = []
        for m in range(2):
            x = q[m * HEAD_DIM:(m + 1) * HEAD_DIM]
            ss = jnp.sum(x * x, axis=0, keepdims=True)
            x = x * lax.rsqrt(ss * (1.0 / HEAD_DIM) + EPS) * gqn
            parts += _rotate_rows(x[0:quarter], x[quarter:2 * quarter], crt, srt)
            parts += _rotate_rows(x[2 * quarter:3 * quarter], x[3 * quarter:HEAD_DIM], cct, sct)
        gqt_ref[c] = (jnp.concatenate(parts, axis=0) * qscale).astype(BF16)
    gvt_ref[...] = rows_t(2 * DIFF_W + GQA_W, KV_W).astype(BF16)

    lo = lax.broadcasted_iota(jnp.int32, (tm, LANES), 1) < HEAD_DIM
    cp, sup, sdp = cp_ref[...], sup_ref[...], sdp_ref[...]
    for c in range(DIFF_HEADS):
        k = proj_k[:, c * LANES:(c + 1) * LANES]
        dk_ref[c] = _rope(k, cp, sup, sdp, half).astype(BF16)
    k = _head_pair_rms(proj_k[:, DIFF_W:DIFF_W + KV_W], gkn_ref[...], lo)
    gk_ref[...] = _rope(k, ca_ref[...], sua_ref[...], sda_ref[...], quarter).astype(BF16)


def _proj(x, P, S):
    T = x.shape[0]
    tm = TOKEN_TILE
    tiles_per_seq = S // tm
    row = lambda i: (i, 0)
    const = lambda i: (0, 0)
    tab_spec = pl.BlockSpec((tm, LANES), lambda i: (i % tiles_per_seq, 0))
    tab_t = lambda n: pl.BlockSpec((n, tm), lambda i: (0, i % tiles_per_seq))
    vec = lambda n: pl.BlockSpec((1, n), const)
    head_t = lambda n, r: pl.BlockSpec((n, r, tm), lambda i: (0, 0, i))
    wk, wt = P["w_in_k"], P["w_in_t"]
    return pl.pallas_call(
        _proj_kernel,
        grid=(T // tm,),
        in_specs=[pl.BlockSpec((tm, D_MODEL), row), vec(D_MODEL),
                  pl.BlockSpec(wk.shape, const), pl.BlockSpec(wt.shape, const)]
                 + [tab_spec] * 6 + [vec(LANES)]
                 + [tab_t(8), tab_t(8), tab_t(16), tab_t(16), tab_t(16), tab_t(16)]
                 + [pl.BlockSpec((HEAD_DIM, tm), const)],
        out_specs=[head_t(DIFF_HEADS, LANES), pl.BlockSpec((DIFF_HEADS, tm, LANES), lambda i: (0, i, 0)),
                   head_t(DIFF_HEADS, LANES), head_t(GQA_HEADS // 2, LANES),
                   pl.BlockSpec((tm, LANES), row), pl.BlockSpec((KV_W, tm), lambda i: (0, i))],
        out_shape=[jax.ShapeDtypeStruct((DIFF_HEADS, LANES, T), BF16),
                   jax.ShapeDtypeStruct((DIFF_HEADS, T, LANES), BF16),
                   jax.ShapeDtypeStruct((DIFF_HEADS, LANES, T), BF16),
                   jax.ShapeDtypeStruct((GQA_HEADS // 2, LANES, T), BF16),
                   jax.ShapeDtypeStruct((T, LANES), BF16),
                   jax.ShapeDtypeStruct((KV_W, T), BF16)],
        compiler_params=pltpu.CompilerParams(dimension_semantics=("arbitrary",), vmem_limit_bytes=VMEM_LIMIT),
        name="proj",
    )(x, P["norm_mix_g"], wk, wt, *P["tabs"], P["gqa_k_norm_g"], *P["tabs_t"], P["gqa_q_norm_g_t"])


def _score_unit(q_rows, hi, k_rows, s_ref, m_ref):
    z = jnp.zeros_like(q_rows)
    lo_pad = jnp.concatenate([q_rows, z], axis=0)
    hi_pad = jnp.concatenate([z, q_rows], axis=0)
    if isinstance(hi, bool):
        qpad = hi_pad if hi else lo_pad
    else:
        qpad = jnp.where(hi, hi_pad, lo_pad)
    n, qb = s_ref.shape
    half = ATTN_SCORE_CHUNK // 2
    acc = []
    for c in range(0, n, ATTN_SCORE_CHUNK):
        s = jnp.dot(k_rows(c, ATTN_SCORE_CHUNK), qpad, preferred_element_type=F32)
        s_ref[c:c + ATTN_SCORE_CHUNK, :] = s
        for r in (0, half):
            acc.append(jnp.max(s[r:r + half].reshape(half // SUBLANES, SUBLANES, qb), axis=0))
    while len(acc) > 1:
        acc = [jnp.maximum(a, b) for a, b in zip(acc[0::2], acc[1::2])]
    m_ref[...] = jnp.max(acc[0], axis=0, keepdims=True)


def _softmax_pv(s_ref, m_ref, vt_cols):
    n, qb = s_ref.shape
    m = m_ref[...]
    o = None
    sums = []
    for c in range(0, n, ATTN_ROW_CHUNK):
        p = jnp.exp2(s_ref[c:c + ATTN_ROW_CHUNK, :] - m)
        sums.append(jnp.sum(p.reshape(ATTN_ROW_CHUNK // SUBLANES, SUBLANES, qb), axis=0))
        oc = jnp.dot(vt_cols(c, ATTN_ROW_CHUNK), p.astype(BF16), preferred_element_type=F32)
        o = oc if o is None else o + oc
    while len(sums) > 1:
        sums = [a + b for a, b in zip(sums[0::2], sums[1::2])]
    l = jnp.sum(sums[0], axis=0, keepdims=True)
    return o / l


def _attn_kernel(*refs, diff):
    if diff:
        (qt_ref, qtn_ref, k_ref, kn_ref, vt_ref, lq1_ref, lk1_ref, lq2_ref, lk2_ref, sg_ref,
         o_ref, s0, s1, m0, m1) = refs
    else:
        qt_ref, qtn_ref, k_ref, kn_ref, vt_ref, o_ref, s0, s1, m0, m1 = refs
    first_step = jnp.logical_and(pl.program_id(0) == 0, pl.program_id(1) == 0)

    def keys(j):
        if diff:
            return lambda c, n: k_ref[j, c:c + n, :]
        return lambda c, n: k_ref[c:c + n, :]

    def next_keys(c, n):
        return kn_ref[0, c:c + n, :] if diff else kn_ref[c:c + n, :]

    def values(j):
        head = j if diff else j // 2
        return lambda c, n: vt_ref[head, :, c:c + n]

    def second_hi(j):
        return True if diff else (j // 2) == 1

    def first_hi(j):
        return False if diff else (j // 2) == 1

    @pl.when(first_step)
    def _():
        _score_unit(qt_ref[0, 0:HEAD_DIM, :], False, keys(0), s0, m0)

    def pair(j, next_q, next_hi, next_k):
        _score_unit(qt_ref[j, HEAD_DIM:2 * HEAD_DIM, :], second_hi(j), keys(j), s1, m1)
        oa = _softmax_pv(s0, m0, values(j))
        _score_unit(next_q, next_hi, next_k, s0, m0)
        ob = _softmax_pv(s1, m1, values(j))
        if diff:
            lam = (jnp.exp(jnp.sum(lq1_ref[...] * lk1_ref[...], axis=-1, keepdims=True))
                   - jnp.exp(jnp.sum(lq2_ref[...] * lk2_ref[...], axis=-1, keepdims=True)) + LAMBDA_INIT)
            d = oa - lam * ob
            ms = jnp.mean(d * d, axis=0, keepdims=True)
            out = d * lax.rsqrt(ms + EPS) * sg_ref[...] * (1.0 - LAMBDA_INIT)
        else:
            out = jnp.concatenate([oa, ob], axis=0)
        o_ref[j] = out.T.astype(BF16)

    for j in range(N_PAIRS - 1):
        pair(j, qt_ref[j + 1, 0:HEAD_DIM, :], first_hi(j + 1), keys(j + 1))
    pair(N_PAIRS - 1, qtn_ref[0, 0:HEAD_DIM, :], False, next_keys)


def _attn(qt, k, vt, P, B, S, diff):
    T = qt.shape[-1]
    qb = ATTN_UNIT_ELEMS // S
    nq = S // qb
    last = B * nq - 1
    cur = lambda b, i: b * nq + i
    nxt = lambda b, i: jnp.minimum(b * nq + i + 1, last)
    const = lambda b, i: (0, 0)
    if diff:
        k_specs = [pl.BlockSpec((DIFF_HEADS, S, LANES), lambda b, i: (0, b, 0)),
                   pl.BlockSpec((1, S, LANES), lambda b, i: (0, nxt(b, i) // nq, 0))]
        vt_spec = pl.BlockSpec((DIFF_HEADS, LANES, S), lambda b, i: (0, 0, b))
        extra = [P["lambda_q1"], P["lambda_k1"], P["lambda_q2"], P["lambda_k2"], P["diff_subln_g_t"]]
        extra_specs = [pl.BlockSpec((1, HEAD_DIM), const)] * 4 + [pl.BlockSpec((LANES, qb), const)]
        name = "diff_attn"
    else:
        k_specs = [pl.BlockSpec((S, LANES), lambda b, i: (b, 0)),
                   pl.BlockSpec((S, LANES), lambda b, i: (nxt(b, i) // nq, 0))]
        vt = vt.reshape(GQA_KV_HEADS, HEAD_DIM, T)
        vt_spec = pl.BlockSpec((GQA_KV_HEADS, HEAD_DIM, S), lambda b, i: (0, 0, b))
        extra, extra_specs = [], []
        name = "gqa_attn"
    return pl.pallas_call(
        functools.partial(_attn_kernel, diff=diff),
        grid=(B, nq),
        in_specs=[pl.BlockSpec((N_PAIRS, LANES, qb), lambda b, i: (0, 0, cur(b, i))),
                  pl.BlockSpec((1, LANES, qb), lambda b, i: (0, 0, nxt(b, i)))] + k_specs + [vt_spec] + extra_specs,
        out_specs=pl.BlockSpec((N_PAIRS, qb, LANES), lambda b, i: (0, cur(b, i), 0)),
        out_shape=jax.ShapeDtypeStruct((N_PAIRS, T, LANES), BF16),
        scratch_shapes=[pltpu.VMEM((S, qb), F32), pltpu.VMEM((S, qb), F32),
                        pltpu.VMEM((1, qb), F32), pltpu.VMEM((1, qb), F32)],
        compiler_params=pltpu.CompilerParams(dimension_semantics=("arbitrary", "arbitrary"),
                                             vmem_limit_bytes=VMEM_LIMIT),
        name=name,
    )(qt, qt, k, k, vt, *extra)


def _memkv_kernel(m_ref, g_ref, w_ref, o_ref):
    m = _rms(m_ref[...], g_ref[...]).astype(BF16)
    o_ref[...] = jnp.dot(m, w_ref[...], preferred_element_type=F32).astype(BF16)


def _memkv(mem, P):
    rows = mem.shape[0]
    return pl.pallas_call(
        _memkv_kernel,
        grid=(rows // N_MEM,),
        in_specs=[pl.BlockSpec((N_MEM, D_MODEL), lambda b: (b, 0)), pl.BlockSpec((1, D_MODEL), lambda b: (0, 0)),
                  pl.BlockSpec((D_MODEL, 2 * D_MODEL), lambda b: (0, 0))],
        out_specs=pl.BlockSpec((N_MEM, 2 * D_MODEL), lambda b: (b, 0)),
        out_shape=jax.ShapeDtypeStruct((rows, 2 * D_MODEL), BF16),
        compiler_params=pltpu.CompilerParams(dimension_semantics=("arbitrary",), vmem_limit_bytes=VMEM_LIMIT),
        name="memkv",
    )(mem, P["norm_mem_g"], P["w_xkv"])


def _post_kernel(x_ref, d_ref, g_ref, mk_ref, mv_ref, wout_ref, gx_ref, wxq_ref, wxo_ref, o_ref):
    mix = jnp.concatenate([d_ref[c] for c in range(N_PAIRS)] + [g_ref[c] for c in range(N_PAIRS)], axis=-1)
    x1 = x_ref[...] + jnp.dot(mix, wout_ref[...], preferred_element_type=F32)
    h = _rms(x1, gx_ref[...]).astype(BF16)
    q = (jnp.dot(h, wxq_ref[...], preferred_element_type=F32) * (XATTN_HEAD_DIM ** -0.5 * LOG2E)).astype(BF16)
    outs = []
    for hh in range(XATTN_HEADS):
        sl = slice(hh * XATTN_HEAD_DIM, (hh + 1) * XATTN_HEAD_DIM)
        s = lax.dot_general(q[:, sl], mk_ref[:, sl], NT_DIMS, preferred_element_type=F32)
        m = jnp.max(s, axis=-1, keepdims=True)
        p = jnp.exp2(s - m)
        l = jnp.sum(p, axis=-1, keepdims=True)
        o = jnp.dot(p.astype(BF16), mv_ref[:, sl], preferred_element_type=F32) / l
        outs.append(o.astype(BF16))
    o = jnp.concatenate(outs, axis=-1)
    o_ref[...] = x1 + jnp.dot(o, wxo_ref[...], preferred_element_type=F32)


def _post(x, d_out, g_out, kv, P, S):
    T = x.shape[0]
    tm = TOKEN_TILE
    tiles_per_seq = S // tm
    row = lambda i: (i, 0)
    const = lambda i: (0, 0)
    heads = pl.BlockSpec((N_PAIRS, tm, LANES), lambda i: (0, i, 0))
    return pl.pallas_call(
        _post_kernel,
        grid=(T // tm,),
        in_specs=[pl.BlockSpec((tm, D_MODEL), row), heads, heads,
                  pl.BlockSpec((N_MEM, D_MODEL), lambda i: (i // tiles_per_seq, 0)),
                  pl.BlockSpec((N_MEM, D_MODEL), lambda i: (i // tiles_per_seq, 1)),
                  pl.BlockSpec((D_MODEL, D_MODEL), const), pl.BlockSpec((1, D_MODEL), const),
                  pl.BlockSpec((D_MODEL, D_MODEL), const), pl.BlockSpec((D_MODEL, D_MODEL), const)],
        out_specs=pl.BlockSpec((tm, D_MODEL), row),
        out_shape=jax.ShapeDtypeStruct((T, D_MODEL), F32),
        compiler_params=pltpu.CompilerParams(dimension_semantics=("arbitrary",), vmem_limit_bytes=VMEM_LIMIT),
        name="post",
    )(x, d_out, g_out, kv, kv, P["w_out"], P["norm_xattn_g"], P["w_xq"], P["w_xo"])


def _ffn_kernel(x_ref, prev_ref, next_ref, gn_ref, wup_ref, cw_ref, cb_ref, wdn_ref, gf_ref, o_ref, act_ref,
                *, tiles_per_seq):
    tm = x_ref.shape[0]
    j = pl.program_id(0) % tiles_per_seq
    has_prev = j > 0
    has_next = j < tiles_per_seq - 1
    x = x_ref[...]
    gn = gn_ref[...]
    h = _rms(x, gn).astype(BF16)
    h_halo = _rms(jnp.concatenate([prev_ref[...], next_ref[...]], axis=0), gn).astype(BF16)
    rid = lax.broadcasted_iota(jnp.int32, (tm, FF_CHUNK), 0)
    is_first = rid == 0
    is_last = rid == tm - 1

    def conv_cols(start):
        w = wup_ref[:, start:start + FF_CHUNK]
        u = jnp.dot(h, w, preferred_element_type=F32)
        uh = jnp.dot(h_halo, w, preferred_element_type=F32)
        prev_row = jnp.where(has_prev, uh[HALO - 1:HALO, :], 0.0)
        next_row = jnp.where(has_next, uh[HALO:HALO + 1, :], 0.0)
        up = jnp.where(is_first, prev_row, pltpu.roll(u, 1, 0))
        un = jnp.where(is_last, next_row, pltpu.roll(u, tm - 1, 0))
        cw = cw_ref[:, start:start + FF_CHUNK]
        return up * cw[0:1, :] + u * cw[1:2, :] + un * cw[2:3, :] + cb_ref[:, start:start + FF_CHUNK]

    for c in range(D_FF // FF_CHUNK):
        a = conv_cols(c * FF_CHUNK)
        b = conv_cols(D_FF + c * FF_CHUNK)
        act_ref[:, c * FF_CHUNK:(c + 1) * FF_CHUNK] = (a / (1.0 + jnp.exp(-a)) * b).astype(BF16)

    x3 = x + jnp.dot(act_ref[...], wdn_ref[...], preferred_element_type=F32)
    o_ref[...] = _rms(x3, gf_ref[...])


def _ffn(x, P, S):
    T = x.shape[0]
    tm = TOKEN_TILE
    halo_per_tile = tm // HALO
    last_halo = T // HALO - 1
    row = lambda i: (i, 0)
    const = lambda i: (0, 0)
    return pl.pallas_call(
        functools.partial(_ffn_kernel, tiles_per_seq=S // tm),
        grid=(T // tm,),
        in_specs=[pl.BlockSpec((tm, D_MODEL), row),
                  pl.BlockSpec((HALO, D_MODEL), lambda i: (jnp.maximum(i * halo_per_tile - 1, 0), 0)),
                  pl.BlockSpec((HALO, D_MODEL), lambda i: (jnp.minimum((i + 1) * halo_per_tile, last_halo), 0)),
                  pl.BlockSpec((1, D_MODEL), const),
                  pl.BlockSpec((D_MODEL, 2 * D_FF), const, pipeline_mode=pl.Buffered(1)),
                  pl.BlockSpec((3, 2 * D_FF), const), pl.BlockSpec((1, 2 * D_FF), const),
                  pl.BlockSpec((D_FF, D_MODEL), const, pipeline_mode=pl.Buffered(1)),
                  pl.BlockSpec((1, D_MODEL), const)],
        out_specs=pl.BlockSpec((tm, D_MODEL), row),
        out_shape=jax.ShapeDtypeStruct((T, D_MODEL), F32),
        scratch_shapes=[pltpu.VMEM((tm, D_FF), BF16)],
        compiler_params=pltpu.CompilerParams(dimension_semantics=("arbitrary",), vmem_limit_bytes=VMEM_LIMIT),
        name="ffn",
    )(x, x, x, P["norm_ffn_g"], P["w_up"], P["conv_w"], P["conv_b"], P["w_down"], P["final_norm_g"])


def _angles(p, dim, theta):
    inv = theta ** (-jnp.arange(0, dim, 2, dtype=F32) / dim)
    return p.astype(F32)[:, None] * inv[None, :]


def _rope_tables(S):
    pos = jnp.arange(S, dtype=jnp.int32)
    lane = jnp.arange(LANES, dtype=jnp.int32) % HEAD_DIM

    half = PARTIAL_ROPE_DIM // 2
    ang = _angles(pos, PARTIAL_ROPE_DIM, ROPE_THETA)[:, lane % half]
    rot = (lane < PARTIAL_ROPE_DIM)[None, :]
    first = ((lane % PARTIAL_ROPE_DIM) < half)[None, :]
    cp = jnp.where(rot, jnp.cos(ang), 1.0)
    sup = jnp.where(rot & first, -jnp.sin(ang), 0.0)
    sdp = jnp.where(rot & ~first, jnp.sin(ang), 0.0)

    half = HEAD_DIM // 4
    ang_r = _angles(pos // GRID_W, HEAD_DIM // 2, AXIAL_THETA)[:, lane % half]
    ang_c = _angles(pos % GRID_W, HEAD_DIM // 2, AXIAL_THETA)[:, lane % half]
    ang = jnp.where((lane < HEAD_DIM // 2)[None, :], ang_r, ang_c)
    first = ((lane % (HEAD_DIM // 2)) < half)[None, :]
    ca = jnp.cos(ang)
    sua = jnp.where(first, -jnp.sin(ang), 0.0)
    sda = jnp.where(first, 0.0, jnp.sin(ang))
    return tuple(t.astype(F32) for t in (cp, sup, sdp, ca, sua, sda))


def _rope_tables_t(S):
    pos = jnp.arange(S, dtype=jnp.int32)
    ang_p = _angles(pos, PARTIAL_ROPE_DIM, ROPE_THETA).T
    ang_r = _angles(pos // GRID_W, HEAD_DIM // 2, AXIAL_THETA).T
    ang_c = _angles(pos % GRID_W, HEAD_DIM // 2, AXIAL_THETA).T
    return (jnp.cos(ang_p), jnp.sin(ang_p), jnp.cos(ang_r), jnp.sin(ang_r), jnp.cos(ang_c), jnp.sin(ang_c))


def _trunk(x, mem, P):
    B, S, _ = x.shape
    xf = x.reshape(B * S, D_MODEL)
    dqt, dk, dvt, gqt, gk, gvt = _proj(xf, P, S)
    d_out = _attn(dqt, dk, dvt, P, B, S, diff=True)
    g_out = _attn(gqt, gk, gvt, P, B, S, diff=False)
    kv = _memkv(mem.reshape(B * N_MEM, D_MODEL), P)
    x2 = _post(xf, d_out, g_out, kv, P, S)
    return _ffn(x2, P, S).reshape(B, S, D_MODEL)


def kernel(x_prompt, x_sample, mem_prompt, mem_sample, norm_mix_g, w_in, lambda_q1, lambda_k1, lambda_q2, lambda_k2, diff_subln_g, gqa_q_norm_g, gqa_k_norm_g, w_out, norm_xattn_g, norm_mem_g, w_xq, w_xkv, w_xo, norm_ffn_g, w_up, conv_w, conv_b, w_down, final_norm_g):
    assert w_in.shape[0] == 1, "single-layer trunk"
    s_max = max(x_prompt.shape[1], x_sample.shape[1])
    qb_max = max(ATTN_UNIT_ELEMS // x_prompt.shape[1], ATTN_UNIT_ELEMS // x_sample.shape[1])
    w = w_in[0].astype(BF16)
    q0, k0, v0, gq0, gk0, gv0, end = 0, 512, 1024, 1536, 2048, 2176, 2304
    w_in_k = jnp.concatenate([w[:, k0:v0], w[:, gk0:gv0]], axis=1)
    w_in_t = jnp.concatenate([w[:, q0:k0], w[:, v0:gq0], w[:, gq0:gk0], w[:, gv0:end]], axis=1).T
    P = {
        "norm_mix_g": norm_mix_g[0][None, :], "w_in_k": w_in_k, "w_in_t": w_in_t,
        "lambda_q1": lambda_q1, "lambda_k1": lambda_k1, "lambda_q2": lambda_q2, "lambda_k2": lambda_k2,
        "diff_subln_g_t": jnp.broadcast_to(diff_subln_g[0][:, None], (LANES, qb_max)),
        "gqa_q_norm_g_t": jnp.broadcast_to(gqa_q_norm_g[0][:, None], (HEAD_DIM, TOKEN_TILE)),
        "gqa_k_norm_g": jnp.concatenate([gqa_k_norm_g, gqa_k_norm_g], axis=-1),
        "w_out": w_out[0].astype(BF16), "norm_xattn_g": norm_xattn_g, "norm_mem_g": norm_mem_g,
        "w_xq": w_xq[0].astype(BF16), "w_xkv": w_xkv[0].astype(BF16), "w_xo": w_xo[0].astype(BF16),
        "norm_ffn_g": norm_ffn_g, "w_up": w_up[0].astype(BF16), "conv_w": conv_w[0], "conv_b": conv_b,
        "w_down": w_down[0].astype(BF16), "final_norm_g": final_norm_g[None, :],
        "tabs": _rope_tables(s_max), "tabs_t": _rope_tables_t(s_max),
    }
    y_prompt = _trunk(x_prompt, mem_prompt, P)
    y_sample = _trunk(x_sample, mem_sample, P)
    return (y_prompt, y_sample)
```

```python
import functools
import math

import jax
import jax.numpy as jnp
from jax import lax
from jax.experimental import pallas as pl
from jax.experimental.pallas import tpu as pltpu

F32 = jnp.float32
BF16 = jnp.bfloat16

D_MODEL = 1024
HEAD_DIM = 64
DIFF_HEADS = 4
GQA_HEADS = 8
GQA_KV_HEADS = 2
DIFF_W = 512
GQA_W = 512
KV_W = GQA_KV_HEADS * HEAD_DIM
ROPE_THETA = 500000.0
PARTIAL_ROPE_DIM = 16
AXIAL_THETA = 10000.0
GRID_W = 64
N_MEM = 256
XATTN_HEADS = 4
XATTN_HEAD_DIM = 256
D_FF = 2816
EPS = 1e-6
LAMBDA_INIT = 0.8 - 0.6 * math.exp(-0.3 * 0)
LOG2E = 1.4426950408889634

LANES = 128
SUBLANES = 8
HALO = SUBLANES
TOKEN_TILE = 512
FF_CHUNK = 256
ATTN_UNIT_ELEMS = 4096 * 256
ATTN_SCORE_CHUNK_ELEMS = 1024 * 256
ATTN_ROW_CHUNK_ELEMS = 512 * 256
N_PAIRS = 4
VMEM_LIMIT = 56 * 1024 * 1024

NT_DIMS = (((1,), (1,)), ((), ()))


def _rms(x, g):
    ms = jnp.mean(x * x, axis=-1, keepdims=True)
    return x * lax.rsqrt(ms + EPS) * g


def _rope(x, c, s_up, s_dn, shift):
    return x * c + pltpu.roll(x, LANES - shift, 1) * s_up + pltpu.roll(x, shift, 1) * s_dn


def _head_pair_rms(x, g, lo):
    sq = x * x
    tot = jnp.sum(sq, axis=-1, keepdims=True)
    first = jnp.sum(jnp.where(lo, sq, 0.0), axis=-1, keepdims=True)
    ss = jnp.where(lo, first, tot - first)
    return x * lax.rsqrt(ss * (1.0 / HEAD_DIM) + EPS) * g


def _rotate_rows(a, b, cos, sin):
    return a * cos - b * sin, a * sin + b * cos


def _proj_kernel(x_ref, g_ref, wk_ref, wt_ref, cp_ref, sup_ref, sdp_ref, ca_ref, sua_ref, sda_ref, gkn_ref,
                 cpt_ref, spt_ref, crt_ref, srt_ref, cct_ref, sct_ref, gqn_ref,
                 dqt_ref, dk_ref, dvt_ref, gqt_ref, gk_ref, gvt_ref):
    tm = x_ref.shape[0]
    h = _rms(x_ref[...], g_ref[...]).astype(BF16)
    qscale = HEAD_DIM ** -0.5 * LOG2E

    proj_k = jnp.dot(h, wk_ref[...], preferred_element_type=F32)
    proj_t = lax.dot_general(wt_ref[...], h, NT_DIMS, preferred_element_type=F32)

    def rows_t(start, n):
        return proj_t[start:start + n]

    cpt, spt = cpt_ref[...], spt_ref[...]
    half = PARTIAL_ROPE_DIM // 2
    for c in range(DIFF_HEADS):
        q = rows_t(c * LANES, LANES)
        parts = []
        for m in range(2):
            b0 = m * HEAD_DIM
            ra, rb = _rotate_rows(q[b0:b0 + half], q[b0 + half:b0 + 2 * half], cpt, spt)
            parts += [ra, rb, q[b0 + 2 * half:b0 + HEAD_DIM]]
        dqt_ref[c] = (jnp.concatenate(parts, axis=0) * qscale).astype(BF16)
        dvt_ref[c] = rows_t(DIFF_W + c * LANES, LANES).astype(BF16)

    crt, srt, cct, sct = crt_ref[...], srt_ref[...], cct_ref[...], sct_ref[...]
    gqn = gqn_ref[...]
    quarter = HEAD_DIM // 4
    for c in range(GQA_HEADS // 2):
        q = rows_t(2 * DIFF_W + c * LANES, LANES)
        parts = []
        for m in range(2):
            x = q[m * HEAD_DIM:(m + 1) * HEAD_DIM]
            ss = jnp.sum(x * x, axis=0, keepdims=True)
            x = x * lax.rsqrt(ss * (1.0 / HEAD_DIM) + EPS) * gqn
            parts += _rotate_rows(x[0:quarter], x[quarter:2 * quarter], crt, srt)
            parts += _rotate_rows(x[2 * quarter:3 * quarter], x[3 * quarter:HEAD_DIM], cct, sct)
        gqt_ref[c] = (jnp.concatenate(parts, axis=0) * qscale).astype(BF16)
    gvt_ref[...] = rows_t(2 * DIFF_W + GQA_W, KV_W).astype(BF16)

    lo = lax.broadcasted_iota(jnp.int32, (tm, LANES), 1) < HEAD_DIM
    cp, sup, sdp = cp_ref[...], sup_ref[...], sdp_ref[...]
    for c in range(DIFF_HEADS):
        k = proj_k[:, c * LANES:(c + 1) * LANES]
        dk_ref[c] = _rope(k, cp, sup, sdp, half).astype(BF16)
    k = _head_pair_rms(proj_k[:, DIFF_W:DIFF_W + KV_W], gkn_ref[...], lo)
    gk_ref[...] = _rope(k, ca_ref[...], sua_ref[...], sda_ref[...], quarter).astype(BF16)


def _proj(x, P, S):
    T = x.shape[0]
    tm = TOKEN_TILE
    tiles_per_seq = S // tm
    row = lambda i: (i, 0)
    const = lambda i: (0, 0)
    tab_spec = pl.BlockSpec((tm, LANES), lambda i: (i % tiles_per_seq, 0))
    tab_t = lambda n: pl.BlockSpec((n, tm), lambda i: (0, i % tiles_per_seq))
    vec = lambda n: pl.BlockSpec((1, n), const)
    head_t = lambda n, r: pl.BlockSpec((n, r, tm), lambda i: (0, 0, i))
    wk, wt = P["w_in_k"], P["w_in_t"]
    return pl.pallas_call(
        _proj_kernel,
        grid=(T // tm,),
        in_specs=[pl.BlockSpec((tm, D_MODEL), row), vec(D_MODEL),
                  pl.BlockSpec(wk.shape, const), pl.BlockSpec(wt.shape, const)]
                 + [tab_spec] * 6 + [vec(LANES)]
                 + [tab_t(8), tab_t(8), tab_t(16), tab_t(16), tab_t(16), tab_t(16)]
                 + [pl.BlockSpec((HEAD_DIM, tm), const)],
        out_specs=[head_t(DIFF_HEADS, LANES), pl.BlockSpec((DIFF_HEADS, tm, LANES), lambda i: (0, i, 0)),
                   head_t(DIFF_HEADS, LANES), head_t(GQA_HEADS // 2, LANES),
                   pl.BlockSpec((tm, LANES), row), pl.BlockSpec((KV_W, tm), lambda i: (0, i))],
        out_shape=[jax.ShapeDtypeStruct((DIFF_HEADS, LANES, T), BF16),
                   jax.ShapeDtypeStruct((DIFF_HEADS, T, LANES), BF16),
                   jax.ShapeDtypeStruct((DIFF_HEADS, LANES, T), BF16),
                   jax.ShapeDtypeStruct((GQA_HEADS // 2, LANES, T), BF16),
                   jax.ShapeDtypeStruct((T, LANES), BF16),
                   jax.ShapeDtypeStruct((KV_W, T), BF16)],
        compiler_params=pltpu.CompilerParams(dimension_semantics=("arbitrary",), vmem_limit_bytes=VMEM_LIMIT),
        name="proj",
    )(x, P["norm_mix_g"], wk, wt, *P["tabs"], P["gqa_k_norm_g"], *P["tabs_t"], P["gqa_q_norm_g_t"])


def _score_unit(q_rows, hi, k_rows, s_ref, m_ref):
    z = jnp.zeros_like(q_rows)
    lo_pad = jnp.concatenate([q_rows, z], axis=0)
    hi_pad = jnp.concatenate([z, q_rows], axis=0)
    if isinstance(hi, bool):
        qpad = hi_pad if hi else lo_pad
    else:
        qpad = jnp.where(hi, hi_pad, lo_pad)
    n, qb = s_ref.shape
    chunk = ATTN_SCORE_CHUNK_ELEMS // qb
    half = chunk // 2
    acc = []
    for c in range(0, n, chunk):
        s = jnp.dot(k_rows(c, chunk), qpad, preferred_element_type=F32)
        s_ref[c:c + chunk, :] = s
        for r in (0, half):
            acc.append(jnp.max(s[r:r + half].reshape(half // SUBLANES, SUBLANES, qb), axis=0))
    while len(acc) > 1:
        acc = [jnp.maximum(a, b) for a, b in zip(acc[0::2], acc[1::2])]
    m_ref[...] = jnp.max(acc[0], axis=0, keepdims=True)


def _softmax_pv(s_ref, m_ref, vt_cols):
    n, qb = s_ref.shape
    chunk = ATTN_ROW_CHUNK_ELEMS // qb
    m = m_ref[...]
    o = None
    sums = []
    for c in range(0, n, chunk):
        p = jnp.exp2(s_ref[c:c + chunk, :] - m)
        sums.append(jnp.sum(p.reshape(chunk // SUBLANES, SUBLANES, qb), axis=0))
        oc = jnp.dot(vt_cols(c, chunk), p.astype(BF16), preferred_element_type=F32)
        o = oc if o is None else o + oc
    while len(sums) > 1:
        sums = [a + b for a, b in zip(sums[0::2], sums[1::2])]
    l = jnp.sum(sums[0], axis=0, keepdims=True)
    return o / l


def _attn_kernel(*refs, diff):
    if diff:
        (qt_ref, qtn_ref, k_ref, kn_ref, vt_ref, lq1_ref, lk1_ref, lq2_ref, lk2_ref, sg_ref,
         o_ref, s0, s1, m0, m1) = refs
    else:
        qt_ref, qtn_ref, k_ref, kn_ref, vt_ref, o_ref, s0, s1, m0, m1 = refs
    first_step = jnp.logical_and(pl.program_id(0) == 0, pl.program_id(1) == 0)

    def keys(j):
        if diff:
            return lambda c, n: k_ref[j, c:c + n, :]
        return lambda c, n: k_ref[c:c + n, :]

    def next_keys(c, n):
        return kn_ref[0, c:c + n, :] if diff else kn_ref[c:c + n, :]

    def values(j):
        head = j if diff else j // 2
        return lambda c, n: vt_ref[head, :, c:c + n]

    def second_hi(j):
        return True if diff else (j // 2) == 1

    def first_hi(j):
        return False if diff else (j // 2) == 1

    @pl.when(first_step)
    def _():
        _score_unit(qt_ref[0, 0:HEAD_DIM, :], False, keys(0), s0, m0)

    def pair(j, next_q, next_hi, next_k):
        _score_unit(qt_ref[j, HEAD_DIM:2 * HEAD_DIM, :], second_hi(j), keys(j), s1, m1)
        oa = _softmax_pv(s0, m0, values(j))
        _score_unit(next_q, next_hi, next_k, s0, m0)
        ob = _softmax_pv(s1, m1, values(j))
        if diff:
            lam = (jnp.exp(jnp.sum(lq1_ref[...] * lk1_ref[...], axis=-1, keepdims=True))
                   - jnp.exp(jnp.sum(lq2_ref[...] * lk2_ref[...], axis=-1, keepdims=True)) + LAMBDA_INIT)
            d = oa - lam * ob
            ms = jnp.mean(d * d, axis=0, keepdims=True)
            out = d * lax.rsqrt(ms + EPS) * sg_ref[...] * (1.0 - LAMBDA_INIT)
        else:
            out = jnp.concatenate([oa, ob], axis=0)
        o_ref[j] = out.T.astype(BF16)

    for j in range(N_PAIRS - 1):
        pair(j, qt_ref[j + 1, 0:HEAD_DIM, :], first_hi(j + 1), keys(j + 1))
    pair(N_PAIRS - 1, qtn_ref[0, 0:HEAD_DIM, :], False, next_keys)


def _attn(qt, k, vt, P, B, S, diff):
    T = qt.shape[-1]
    qb = ATTN_UNIT_ELEMS // S
    nq = S // qb
    last = B * nq - 1
    cur = lambda b, i: b * nq + i
    nxt = lambda b, i: jnp.minimum(b * nq + i + 1, last)
    const = lambda b, i: (0, 0)
    if diff:
        k_specs = [pl.BlockSpec((DIFF_HEADS, S, LANES), lambda b, i: (0, b, 0)),
                   pl.BlockSpec((1, S, LANES), lambda b, i: (0, nxt(b, i) // nq, 0))]
        vt_spec = pl.BlockSpec((DIFF_HEADS, LANES, S), lambda b, i: (0, 0, b))
        extra = [P["lambda_q1"], P["lambda_k1"], P["lambda_q2"], P["lambda_k2"], P["diff_subln_g_t"]]
        extra_specs = [pl.BlockSpec((1, HEAD_DIM), const)] * 4 + [pl.BlockSpec((LANES, qb), const)]
        name = "diff_attn"
    else:
        k_specs = [pl.BlockSpec((S, LANES), lambda b, i: (b, 0)),
                   pl.BlockSpec((S, LANES), lambda b, i: (nxt(b, i) // nq, 0))]
        vt = vt.reshape(GQA_KV_HEADS, HEAD_DIM, T)
        vt_spec = pl.BlockSpec((GQA_KV_HEADS, HEAD_DIM, S), lambda b, i: (0, 0, b))
        extra, extra_specs = [], []
        name = "gqa_attn"
    return pl.pallas_call(
        functools.partial(_attn_kernel, diff=diff),
        grid=(B, nq),
        in_specs=[pl.BlockSpec((N_PAIRS, LANES, qb), lambda b, i: (0, 0, cur(b, i))),
                  pl.BlockSpec((1, LANES, qb), lambda b, i: (0, 0, nxt(b, i)))] + k_specs + [vt_spec] + extra_specs,
        out_specs=pl.BlockSpec((N_PAIRS, qb, LANES), lambda b, i: (0, cur(b, i), 0)),
        out_shape=jax.ShapeDtypeStruct((N_PAIRS, T, LANES), BF16),
        scratch_shapes=[pltpu.VMEM((S, qb), F32), pltpu.VMEM((S, qb), F32),
                        pltpu.VMEM((1, qb), F32), pltpu.VMEM((1, qb), F32)],
        compiler_params=pltpu.CompilerParams(dimension_semantics=("arbitrary", "arbitrary"),
                                             vmem_limit_bytes=VMEM_LIMIT),
        name=name,
    )(qt, qt, k, k, vt, *extra)


def _memkv_kernel(m_ref, g_ref, w_ref, o_ref):
    m = _rms(m_ref[...], g_ref[...]).astype(BF16)
    o_ref[...] = jnp.dot(m, w_ref[...], preferred_element_type=F32).astype(BF16)


def _memkv(mem, P):
    rows = mem.shape[0]
    return pl.pallas_call(
        _memkv_kernel,
        grid=(rows // N_MEM,),
        in_specs=[pl.BlockSpec((N_MEM, D_MODEL), lambda b: (b, 0)), pl.BlockSpec((1, D_MODEL), lambda b: (0, 0)),
                  pl.BlockSpec((D_MODEL, 2 * D_MODEL), lambda b: (0, 0))],
        out_specs=pl.BlockSpec((N_MEM, 2 * D_MODEL), lambda b: (b, 0)),
        out_shape=jax.ShapeDtypeStruct((rows, 2 * D_MODEL), BF16),
        compiler_params=pltpu.CompilerParams(dimension_semantics=("arbitrary",), vmem_limit_bytes=VMEM_LIMIT),
        name="memkv",
    )(mem, P["norm_mem_g"], P["w_xkv"])


def _post_kernel(x_ref, d_ref, g_ref, mk_ref, mv_ref, wout_ref, gx_ref, wxq_ref, wxo_ref, o_ref):
    mix = jnp.concatenate([d_ref[c] for c in range(N_PAIRS)] + [g_ref[c] for c in range(N_PAIRS)], axis=-1)
    x1 = x_ref[...] + jnp.dot(mix, wout_ref[...], preferred_element_type=F32)
    h = _rms(x1, gx_ref[...]).astype(BF16)
    q = (jnp.dot(h, wxq_ref[...], preferred_element_type=F32) * (XATTN_HEAD_DIM ** -0.5 * LOG2E)).astype(BF16)
    outs = []
    for hh in range(XATTN_HEADS):
        sl = slice(hh * XATTN_HEAD_DIM, (hh + 1) * XATTN_HEAD_DIM)
        s = lax.dot_general(q[:, sl], mk_ref[:, sl], NT_DIMS, preferred_element_type=F32)
        m = jnp.max(s, axis=-1, keepdims=True)
        p = jnp.exp2(s - m)
        l = jnp.sum(p, axis=-1, keepdims=True)
        o = jnp.dot(p.astype(BF16), mv_ref[:, sl], preferred_element_type=F32) / l
        outs.append(o.astype(BF16))
    o = jnp.concatenate(outs, axis=-1)
    o_ref[...] = x1 + jnp.dot(o, wxo_ref[...], preferred_element_type=F32)


def _post(x, d_out, g_out, kv, P, S):
    T = x.shape[0]
    tm = TOKEN_TILE
    tiles_per_seq = S // tm
    row = lambda i: (i, 0)
    const = lambda i: (0, 0)
    heads = pl.BlockSpec((N_PAIRS, tm, LANES), lambda i: (0, i, 0))
    return pl.pallas_call(
        _post_kernel,
        grid=(T // tm,),
        in_specs=[pl.BlockSpec((tm, D_MODEL), row), heads, heads,
                  pl.BlockSpec((N_MEM, D_MODEL), lambda i: (i // tiles_per_seq, 0)),
                  pl.BlockSpec((N_MEM, D_MODEL), lambda i: (i // tiles_per_seq, 1)),
                  pl.BlockSpec((D_MODEL, D_MODEL), const), pl.BlockSpec((1, D_MODEL), const),
                  pl.BlockSpec((D_MODEL, D_MODEL), const), pl.BlockSpec((D_MODEL, D_MODEL), const)],
        out_specs=pl.BlockSpec((tm, D_MODEL), row),
        out_shape=jax.ShapeDtypeStruct((T, D_MODEL), F32),
        compiler_params=pltpu.CompilerParams(dimension_semantics=("arbitrary",), vmem_limit_bytes=VMEM_LIMIT),
        name="post",
    )(x, d_out, g_out, kv, kv, P["w_out"], P["norm_xattn_g"], P["w_xq"], P["w_xo"])


def _ffn_kernel(x_ref, prev_ref, next_ref, gn_ref, wup_ref, cw_ref, cb_ref, wdn_ref, gf_ref, o_ref, act_ref,
                *, tiles_per_seq):
    tm = x_ref.shape[0]
    j = pl.program_id(0) % tiles_per_seq
    has_prev = j > 0
    has_next = j < tiles_per_seq - 1
    x = x_ref[...]
    gn = gn_ref[...]
    rows = tm + 2 * HALO
    h_prev = jnp.where(has_prev, _rms(prev_ref[...], gn), 0.0)
    h_next = jnp.where(has_next, _rms(next_ref[...], gn), 0.0)
    h = jnp.concatenate([h_prev, _rms(x, gn), h_next], axis=0).astype(BF16)

    def conv_cols(start):
        u = jnp.dot(h, wup_ref[:, start:start + FF_CHUNK], preferred_element_type=F32)
        cw = cw_ref[:, start:start + FF_CHUNK]
        y = (pltpu.roll(u, 1, 0) * cw[0:1, :] + u * cw[1:2, :] + pltpu.roll(u, rows - 1, 0) * cw[2:3, :]
             + cb_ref[:, start:start + FF_CHUNK])
        return y[HALO:HALO + tm]

    for c in range(D_FF // FF_CHUNK):
        a = conv_cols(c * FF_CHUNK)
        b = conv_cols(D_FF + c * FF_CHUNK)
        act_ref[:, c * FF_CHUNK:(c + 1) * FF_CHUNK] = (a / (1.0 + jnp.exp(-a)) * b).astype(BF16)

    x3 = x + jnp.dot(act_ref[...], wdn_ref[...], preferred_element_type=F32)
    o_ref[...] = _rms(x3, gf_ref[...])


def _ffn(x, P, S):
    T = x.shape[0]
    tm = TOKEN_TILE
    halo_per_tile = tm // HALO
    last_halo = T // HALO - 1
    row = lambda i: (i, 0)
    const = lambda i: (0, 0)
    return pl.pallas_call(
        functools.partial(_ffn_kernel, tiles_per_seq=S // tm),
        grid=(T // tm,),
        in_specs=[pl.BlockSpec((tm, D_MODEL), row),
                  pl.BlockSpec((HALO, D_MODEL), lambda i: (jnp.maximum(i * halo_per_tile - 1, 0), 0)),
                  pl.BlockSpec((HALO, D_MODEL), lambda i: (jnp.minimum((i + 1) * halo_per_tile, last_halo), 0)),
                  pl.BlockSpec((1, D_MODEL), const),
                  pl.BlockSpec((D_MODEL, 2 * D_FF), const, pipeline_mode=pl.Buffered(1)),
                  pl.BlockSpec((3, 2 * D_FF), const), pl.BlockSpec((1, 2 * D_FF), const),
                  pl.BlockSpec((D_FF, D_MODEL), const, pipeline_mode=pl.Buffered(1)),
                  pl.BlockSpec((1, D_MODEL), const)],
        out_specs=pl.BlockSpec((tm, D_MODEL), row),
        out_shape=jax.ShapeDtypeStruct((T, D_MODEL), F32),
        scratch_shapes=[pltpu.VMEM((tm, D_FF), BF16)],
        compiler_params=pltpu.CompilerParams(dimension_semantics=("arbitrary",), vmem_limit_bytes=VMEM_LIMIT),
        name="ffn",
    )(x, x, x, P["norm_ffn_g"], P["w_up"], P["conv_w"], P["conv_b"], P["w_down"], P["final_norm_g"])


def _angles(p, dim, theta):
    inv = theta ** (-jnp.arange(0, dim, 2, dtype=F32) / dim)
    return p.astype(F32)[:, None] * inv[None, :]


def _rope_tables(S):
    pos = jnp.arange(S, dtype=jnp.int32)
    lane = jnp.arange(LANES, dtype=jnp.int32) % HEAD_DIM

    half = PARTIAL_ROPE_DIM // 2
    ang = _angles(pos, PARTIAL_ROPE_DIM, ROPE_THETA)[:, lane % half]
    rot = (lane < PARTIAL_ROPE_DIM)[None, :]
    first = ((lane % PARTIAL_ROPE_DIM) < half)[None, :]
    cp = jnp.where(rot, jnp.cos(ang), 1.0)
    sup = jnp.where(rot & first, -jnp.sin(ang), 0.0)
    sdp = jnp.where(rot & ~first, jnp.sin(ang), 0.0)

    half = HEAD_DIM // 4
    ang_r = _angles(pos // GRID_W, HEAD_DIM // 2, AXIAL_THETA)[:, lane % half]
    ang_c = _angles(pos % GRID_W, HEAD_DIM // 2, AXIAL_THETA)[:, lane % half]
    ang = jnp.where((lane < HEAD_DIM // 2)[None, :], ang_r, ang_c)
    first = ((lane % (HEAD_DIM // 2)) < half)[None, :]
    ca = jnp.cos(ang)
    sua = jnp.where(first, -jnp.sin(ang), 0.0)
    sda = jnp.where(first, 0.0, jnp.sin(ang))
    return tuple(t.astype(F32) for t in (cp, sup, sdp, ca, sua, sda))


def _rope_tables_t(S):
    pos = jnp.arange(S, dtype=jnp.int32)
    ang_p = _angles(pos, PARTIAL_ROPE_DIM, ROPE_THETA).T
    ang_r = _angles(pos // GRID_W, HEAD_DIM // 2, AXIAL_THETA).T
    ang_c = _angles(pos % GRID_W, HEAD_DIM // 2, AXIAL_THETA).T
    return (jnp.cos(ang_p), jnp.sin(ang_p), jnp.cos(ang_r), jnp.sin(ang_r), jnp.cos(ang_c), jnp.sin(ang_c))


def _trunk(x, mem, P):
    B, S, _ = x.shape
    xf = x.reshape(B * S, D_MODEL)
    dqt, dk, dvt, gqt, gk, gvt = _proj(xf, P, S)
    d_out = _attn(dqt, dk, dvt, P, B, S, diff=True)
    g_out = _attn(gqt, gk, gvt, P, B, S, diff=False)
    kv = _memkv(mem.reshape(B * N_MEM, D_MODEL), P)
    x2 = _post(xf, d_out, g_out, kv, P, S)
    return _ffn(x2, P, S).reshape(B, S, D_MODEL)


def kernel(x_prompt, x_sample, mem_prompt, mem_sample, norm_mix_g, w_in, lambda_q1, lambda_k1, lambda_q2, lambda_k2, diff_subln_g, gqa_q_norm_g, gqa_k_norm_g, w_out, norm_xattn_g, norm_mem_g, w_xq, w_xkv, w_xo, norm_ffn_g, w_up, conv_w, conv_b, w_down, final_norm_g):
    assert w_in.shape[0] == 1, "single-layer trunk"
    s_max = max(x_prompt.shape[1], x_sample.shape[1])
    qb_max = max(ATTN_UNIT_ELEMS // x_prompt.shape[1], ATTN_UNIT_ELEMS // x_sample.shape[1])
    w = w_in[0].astype(BF16)
    q0, k0, v0, gq0, gk0, gv0, end = 0, 512, 1024, 1536, 2048, 2176, 2304
    w_in_k = jnp.concatenate([w[:, k0:v0], w[:, gk0:gv0]], axis=1)
    w_in_t = jnp.concatenate([w[:, q0:k0], w[:, v0:gq0], w[:, gq0:gk0], w[:, gv0:end]], axis=1).T
    P = {
        "norm_mix_g": norm_mix_g[0][None, :], "w_in_k": w_in_k, "w_in_t": w_in_t,
        "lambda_q1": lambda_q1, "lambda_k1": lambda_k1, "lambda_q2": lambda_q2, "lambda_k2": lambda_k2,
        "diff_subln_g_t": jnp.broadcast_to(diff_subln_g[0][:, None], (LANES, qb_max)),
        "gqa_q_norm_g_t": jnp.broadcast_to(gqa_q_norm_g[0][:, None], (HEAD_DIM, TOKEN_TILE)),
        "gqa_k_norm_g": jnp.concatenate([gqa_k_norm_g, gqa_k_norm_g], axis=-1),
        "w_out": w_out[0].astype(BF16), "norm_xattn_g": norm_xattn_g, "norm_mem_g": norm_mem_g,
        "w_xq": w_xq[0].astype(BF16), "w_xkv": w_xkv[0].astype(BF16), "w_xo": w_xo[0].astype(BF16),
        "norm_ffn_g": norm_ffn_g, "w_up": w_up[0].astype(BF16), "conv_w": conv_w[0], "conv_b": conv_b,
        "w_down": w_down[0].astype(BF16), "final_norm_g": final_norm_g[None, :],
        "tabs": _rope_tables(s_max), "tabs_t": _rope_tables_t(s_max),
    }
    y_prompt = _trunk(x_prompt, mem_prompt, P)
    y_sample = _trunk(x_sample, mem_sample, P)
    return (y_prompt, y_sample)
```

```python
import functools
import math

import jax
import jax.numpy as jnp
from jax import lax
from jax.experimental import pallas as pl
from jax.experimental.pallas import tpu as pltpu

F32 = jnp.float32
BF16 = jnp.bfloat16

D_MODEL = 1024
HEAD_DIM = 64
DIFF_HEADS = 4
GQA_HEADS = 8
GQA_KV_HEADS = 2
DIFF_W = 512
GQA_W = 512
KV_W = GQA_KV_HEADS * HEAD_DIM
ROPE_THETA = 500000.0
PARTIAL_ROPE_DIM = 16
AXIAL_THETA = 10000.0
GRID_W = 64
N_MEM = 256
XATTN_HEADS = 4
XATTN_HEAD_DIM = 256
D_FF = 2816
EPS = 1e-6
LAMBDA_INIT = 0.8 - 0.6 * math.exp(-0.3 * 0)
LOG2E = 1.4426950408889634

LANES = 128
SUBLANES = 8
HALO = SUBLANES
TOKEN_TILE = 512
FF_CHUNK = 256
ATTN_UNIT_ELEMS = 4096 * 256
ATTN_SCORE_CHUNK_ELEMS = 1024 * 256
ATTN_NARROW_QB = 256
N_PAIRS = 4
VMEM_LIMIT = 56 * 1024 * 1024

NT_DIMS = (((1,), (1,)), ((), ()))


def _rms(x, g):
    ms = jnp.mean(x * x, axis=-1, keepdims=True)
    return x * lax.rsqrt(ms + EPS) * g


def _rope(x, c, s_up, s_dn, shift):
    return x * c + pltpu.roll(x, LANES - shift, 1) * s_up + pltpu.roll(x, shift, 1) * s_dn


def _head_pair_rms(x, g, lo):
    sq = x * x
    tot = jnp.sum(sq, axis=-1, keepdims=True)
    first = jnp.sum(jnp.where(lo, sq, 0.0), axis=-1, keepdims=True)
    ss = jnp.where(lo, first, tot - first)
    return x * lax.rsqrt(ss * (1.0 / HEAD_DIM) + EPS) * g


def _rotate_rows(a, b, cos, sin):
    return a * cos - b * sin, a * sin + b * cos


def _proj_kernel(x_ref, g_ref, wk_ref, wt_ref, cp_ref, sup_ref, sdp_ref, ca_ref, sua_ref, sda_ref, gkn_ref,
                 cpt_ref, spt_ref, crt_ref, srt_ref, cct_ref, sct_ref, gqn_ref,
                 dqt_ref, dk_ref, dvt_ref, gqt_ref, gk_ref, gvt_ref):
    tm = x_ref.shape[0]
    h = _rms(x_ref[...], g_ref[...]).astype(BF16)
    qscale = HEAD_DIM ** -0.5 * LOG2E

    proj_k = jnp.dot(h, wk_ref[...], preferred_element_type=F32)
    proj_t = lax.dot_general(wt_ref[...], h, NT_DIMS, preferred_element_type=F32)

    def rows_t(start, n):
        return proj_t[start:start + n]

    cpt, spt = cpt_ref[...], spt_ref[...]
    half = PARTIAL_ROPE_DIM // 2
    for c in range(DIFF_HEADS):
        q = rows_t(c * LANES, LANES)
        parts = []
        for m in range(2):
            b0 = m * HEAD_DIM
            ra, rb = _rotate_rows(q[b0:b0 + half], q[b0 + half:b0 + 2 * half], cpt, spt)
            parts += [ra, rb, q[b0 + 2 * half:b0 + HEAD_DIM]]
        dqt_ref[c] = (jnp.concatenate(parts, axis=0) * qscale).astype(BF16)
        dvt_ref[c] = rows_t(DIFF_W + c * LANES, LANES).astype(BF16)

    crt, srt, cct, sct = crt_ref[...], srt_ref[...], cct_ref[...], sct_ref[...]
    gqn = gqn_ref[...]
    quarter = HEAD_DIM // 4
    for c in range(GQA_HEADS // 2):
        q = rows_t(2 * DIFF_W + c * LANES, LANES)
        parts = []
        for m in range(2):
            x = q[m * HEAD_DIM:(m + 1) * HEAD_DIM]
            ss = jnp.sum(x * x, axis=0, keepdims=True)
            x = x * lax.rsqrt(ss * (1.0 / HEAD_DIM) + EPS) * gqn
            parts += _rotate_rows(x[0:quarter], x[quarter:2 * quarter], crt, srt)
            parts += _rotate_rows(x[2 * quarter:3 * quarter], x[3 * quarter:HEAD_DIM], cct, sct)
        gqt_ref[c] = (jnp.concatenate(parts, axis=0) * qscale).astype(BF16)
    gvt_ref[...] = rows_t(2 * DIFF_W + GQA_W, KV_W).astype(BF16)

    lo = lax.broadcasted_iota(jnp.int32, (tm, LANES), 1) < HEAD_DIM
    cp, sup, sdp = cp_ref[...], sup_ref[...], sdp_ref[...]
    for c in range(DIFF_HEADS):
        k = proj_k[:, c * LANES:(c + 1) * LANES]
        dk_ref[c] = _rope(k, cp, sup, sdp, half).astype(BF16)
    k = _head_pair_rms(proj_k[:, DIFF_W:DIFF_W + KV_W], gkn_ref[...], lo)
    gk_ref[...] = _rope(k, ca_ref[...], sua_ref[...], sda_ref[...], quarter).astype(BF16)


def _proj(x, P, S):
    T = x.shape[0]
    tm = TOKEN_TILE
    tiles_per_seq = S // tm
    row = lambda i: (i, 0)
    const = lambda i: (0, 0)
    tab_spec = pl.BlockSpec((tm, LANES), lambda i: (i % tiles_per_seq, 0))
    tab_t = lambda n: pl.BlockSpec((n, tm), lambda i: (0, i % tiles_per_seq))
    vec = lambda n: pl.BlockSpec((1, n), const)
    head_t = lambda n, r: pl.BlockSpec((n, r, tm), lambda i: (0, 0, i))
    wk, wt = P["w_in_k"], P["w_in_t"]
    return pl.pallas_call(
        _proj_kernel,
        grid=(T // tm,),
        in_specs=[pl.BlockSpec((tm, D_MODEL), row), vec(D_MODEL),
                  pl.BlockSpec(wk.shape, const), pl.BlockSpec(wt.shape, const)]
                 + [tab_spec] * 6 + [vec(LANES)]
                 + [tab_t(8), tab_t(8), tab_t(16), tab_t(16), tab_t(16), tab_t(16)]
                 + [pl.BlockSpec((HEAD_DIM, tm), const)],
        out_specs=[head_t(DIFF_HEADS, LANES), pl.BlockSpec((DIFF_HEADS, tm, LANES), lambda i: (0, i, 0)),
                   head_t(DIFF_HEADS, LANES), head_t(GQA_HEADS // 2, LANES),
                   pl.BlockSpec((tm, LANES), row), pl.BlockSpec((KV_W, tm), lambda i: (0, i))],
        out_shape=[jax.ShapeDtypeStruct((DIFF_HEADS, LANES, T), BF16),
                   jax.ShapeDtypeStruct((DIFF_HEADS, T, LANES), BF16),
                   jax.ShapeDtypeStruct((DIFF_HEADS, LANES, T), BF16),
                   jax.ShapeDtypeStruct((GQA_HEADS // 2, LANES, T), BF16),
                   jax.ShapeDtypeStruct((T, LANES), BF16),
                   jax.ShapeDtypeStruct((KV_W, T), BF16)],
        compiler_params=pltpu.CompilerParams(dimension_semantics=("arbitrary",), vmem_limit_bytes=VMEM_LIMIT),
        name="proj",
    )(x, P["norm_mix_g"], wk, wt, *P["tabs"], P["gqa_k_norm_g"], *P["tabs_t"], P["gqa_q_norm_g_t"])


def _pad_queries(q_rows, hi):
    z = jnp.zeros_like(q_rows)
    return jnp.concatenate([z, q_rows] if hi else [q_rows, z], axis=0)


def _column_max(chunks):
    while len(chunks) > 1:
        chunks = [jnp.maximum(a, b) for a, b in zip(chunks[0::2], chunks[1::2])]
    return jnp.max(chunks[0], axis=0, keepdims=True)


def _score_chunk(k_rows, qpad, s_ref, c, rows, acc):
    s = jnp.dot(k_rows(c, rows), qpad, preferred_element_type=F32)
    s_ref[c:c + rows, :] = s
    half = rows // 2
    for r in (0, half):
        acc.append(jnp.max(s[r:r + half].reshape(half // SUBLANES, SUBLANES, s.shape[1]), axis=0))


def _score_unit(q_rows, hi, k_rows, s_ref, m_ref):
    n, qb = s_ref.shape
    rows = ATTN_SCORE_CHUNK_ELEMS // qb
    qpad = _pad_queries(q_rows, hi)
    acc = []
    for c in range(0, n, rows):
        _score_chunk(k_rows, qpad, s_ref, c, rows, acc)
    m_ref[...] = _column_max(acc)


def _phase(q_rows, hi, k_rows, s_nxt, m_nxt, s_cur, m_cur, vt_cols):
    n, qb = s_cur.shape
    a_rows = ATTN_SCORE_CHUNK_ELEMS // qb
    b_rows = a_rows // 2
    interleave = qb > ATTN_NARROW_QB
    qpad = _pad_queries(q_rows, hi)
    m = m_cur[...]
    acc, sums, pending = [], [], []
    o = None

    def value_matmuls(o):
        for c0, p in pending:
            oc = jnp.dot(vt_cols(c0, b_rows), p, preferred_element_type=F32)
            o = oc if o is None else o + oc
        return o

    def softmax_chunk(r):
        p = jnp.exp2(s_cur[r:r + b_rows, :] - m)
        sums.append(jnp.sum(p.reshape(b_rows // SUBLANES, SUBLANES, qb), axis=0))
        return r, p.astype(BF16)

    for c in range(0, n, a_rows):
        _score_chunk(k_rows, qpad, s_nxt, c, a_rows, acc)
        if interleave:
            o = value_matmuls(o)
            pending = [softmax_chunk(r) for r in (c, c + b_rows)]
    if not interleave:
        for r in range(0, n, b_rows):
            pending = [softmax_chunk(r)]
            o = value_matmuls(o)
    else:
        o = value_matmuls(o)
    m_nxt[...] = _column_max(acc)
    while len(sums) > 1:
        sums = [a + b for a, b in zip(sums[0::2], sums[1::2])]
    return o / jnp.sum(sums[0], axis=0, keepdims=True)


def _attn_kernel(*refs, diff):
    if diff:
        (qt_ref, qtn_ref, k_ref, kn_ref, vt_ref, lq1_ref, lk1_ref, lq2_ref, lk2_ref, sg_ref,
         o_ref, s0, s1, m0, m1) = refs
        lam = (jnp.exp(jnp.sum(lq1_ref[...] * lk1_ref[...], axis=-1, keepdims=True))
               - jnp.exp(jnp.sum(lq2_ref[...] * lk2_ref[...], axis=-1, keepdims=True)) + LAMBDA_INIT)
    else:
        qt_ref, qtn_ref, k_ref, kn_ref, vt_ref, o_ref, s0, s1, m0, m1 = refs

    def keys(j):
        return (lambda c, r: k_ref[j, c:c + r, :]) if diff else (lambda c, r: k_ref[c:c + r, :])

    def next_keys(c, r):
        return kn_ref[0, c:c + r, :] if diff else kn_ref[c:c + r, :]

    def values(j):
        head = j if diff else j // 2
        return lambda c, r: vt_ref[head, :, c:c + r]

    def his(j):
        return (False, True) if diff else (j // 2 == 1, j // 2 == 1)

    @pl.when(jnp.logical_and(pl.program_id(0) == 0, pl.program_id(1) == 0))
    def _():
        _score_unit(qt_ref[0, 0:HEAD_DIM, :], False, keys(0), s0, m0)

    for j in range(N_PAIRS):
        oa = _phase(qt_ref[j, HEAD_DIM:2 * HEAD_DIM, :], his(j)[1], keys(j), s1, m1, s0, m0, values(j))
        if j + 1 < N_PAIRS:
            nxt = (qt_ref[j + 1, 0:HEAD_DIM, :], his(j + 1)[0], keys(j + 1))
        else:
            nxt = (qtn_ref[0, 0:HEAD_DIM, :], False, next_keys)
        ob = _phase(*nxt, s0, m0, s1, m1, values(j))
        if diff:
            d = oa - lam * ob
            ms = jnp.mean(d * d, axis=0, keepdims=True)
            out = d * lax.rsqrt(ms + EPS) * sg_ref[...] * (1.0 - LAMBDA_INIT)
        else:
            out = jnp.concatenate([oa, ob], axis=0)
        o_ref[j] = out.T.astype(BF16)


def _attn(qt, k, vt, P, B, S, diff):
    T = qt.shape[-1]
    qb = ATTN_UNIT_ELEMS // S
    nq = S // qb
    last = B * nq - 1
    cur = lambda b, i: b * nq + i
    nxt = lambda b, i: jnp.minimum(b * nq + i + 1, last)
    const = lambda b, i: (0, 0)
    if diff:
        k_specs = [pl.BlockSpec((DIFF_HEADS, S, LANES), lambda b, i: (0, b, 0)),
                   pl.BlockSpec((1, S, LANES), lambda b, i: (0, nxt(b, i) // nq, 0))]
        vt_spec = pl.BlockSpec((DIFF_HEADS, LANES, S), lambda b, i: (0, 0, b))
        extra = [P["lambda_q1"], P["lambda_k1"], P["lambda_q2"], P["lambda_k2"], P["diff_subln_g_t"]]
        extra_specs = [pl.BlockSpec((1, HEAD_DIM), const)] * 4 + [pl.BlockSpec((LANES, qb), const)]
        name = "diff_attn"
    else:
        k_specs = [pl.BlockSpec((S, LANES), lambda b, i: (b, 0)),
                   pl.BlockSpec((S, LANES), lambda b, i: (nxt(b, i) // nq, 0))]
        vt = vt.reshape(GQA_KV_HEADS, HEAD_DIM, T)
        vt_spec = pl.BlockSpec((GQA_KV_HEADS, HEAD_DIM, S), lambda b, i: (0, 0, b))
        extra, extra_specs = [], []
        name = "gqa_attn"
    return pl.pallas_call(
        functools.partial(_attn_kernel, diff=diff),
        grid=(B, nq),
        in_specs=[pl.BlockSpec((N_PAIRS, LANES, qb), lambda b, i: (0, 0, cur(b, i))),
                  pl.BlockSpec((1, LANES, qb), lambda b, i: (0, 0, nxt(b, i)))] + k_specs + [vt_spec] + extra_specs,
        out_specs=pl.BlockSpec((N_PAIRS, qb, LANES), lambda b, i: (0, cur(b, i), 0)),
        out_shape=jax.ShapeDtypeStruct((N_PAIRS, T, LANES), BF16),
        scratch_shapes=[pltpu.VMEM((S, qb), F32), pltpu.VMEM((S, qb), F32),
                        pltpu.VMEM((1, qb), F32), pltpu.VMEM((1, qb), F32)],
        compiler_params=pltpu.CompilerParams(dimension_semantics=("arbitrary", "arbitrary"),
                                             vmem_limit_bytes=VMEM_LIMIT),
        name=name,
    )(qt, qt, k, k, vt, *extra)


def _post_kernel(x_ref, d_ref, g_ref, mem_ref, gm_ref, wkv_ref, wout_ref, gx_ref, wxq_ref, wxo_ref, o_ref, kv_ref,
                 *, tiles_per_seq):
    @pl.when(pl.program_id(0) % tiles_per_seq == 0)
    def _():
        m = _rms(mem_ref[...], gm_ref[...]).astype(BF16)
        kv_ref[...] = jnp.dot(m, wkv_ref[...], preferred_element_type=F32).astype(BF16)

    mix = jnp.concatenate([d_ref[c] for c in range(N_PAIRS)] + [g_ref[c] for c in range(N_PAIRS)], axis=-1)
    x1 = x_ref[...] + jnp.dot(mix, wout_ref[...], preferred_element_type=F32)
    h = _rms(x1, gx_ref[...]).astype(BF16)
    q = (jnp.dot(h, wxq_ref[...], preferred_element_type=F32) * (XATTN_HEAD_DIM ** -0.5 * LOG2E)).astype(BF16)
    outs = []
    for hh in range(XATTN_HEADS):
        sl = slice(hh * XATTN_HEAD_DIM, (hh + 1) * XATTN_HEAD_DIM)
        vsl = slice(D_MODEL + hh * XATTN_HEAD_DIM, D_MODEL + (hh + 1) * XATTN_HEAD_DIM)
        s = lax.dot_general(q[:, sl], kv_ref[:, sl], NT_DIMS, preferred_element_type=F32)
        m = jnp.max(s, axis=-1, keepdims=True)
        p = jnp.exp2(s - m)
        l = jnp.sum(p, axis=-1, keepdims=True)
        o = jnp.dot(p.astype(BF16), kv_ref[:, vsl], preferred_element_type=F32) / l
        outs.append(o.astype(BF16))
    o = jnp.concatenate(outs, axis=-1)
    o_ref[...] = x1 + jnp.dot(o, wxo_ref[...], preferred_element_type=F32)


def _post(x, d_out, g_out, mem, P, S):
    T = x.shape[0]
    tm = TOKEN_TILE
    tiles_per_seq = S // tm
    row = lambda i: (i, 0)
    const = lambda i: (0, 0)
    heads = pl.BlockSpec((N_PAIRS, tm, LANES), lambda i: (0, i, 0))
    return pl.pallas_call(
        functools.partial(_post_kernel, tiles_per_seq=tiles_per_seq),
        grid=(T // tm,),
        in_specs=[pl.BlockSpec((tm, D_MODEL), row), heads, heads,
                  pl.BlockSpec((N_MEM, D_MODEL), lambda i: (i // tiles_per_seq, 0)),
                  pl.BlockSpec((1, D_MODEL), const), pl.BlockSpec((D_MODEL, 2 * D_MODEL), const),
                  pl.BlockSpec((D_MODEL, D_MODEL), const), pl.BlockSpec((1, D_MODEL), const),
                  pl.BlockSpec((D_MODEL, D_MODEL), const), pl.BlockSpec((D_MODEL, D_MODEL), const)],
        out_specs=pl.BlockSpec((tm, D_MODEL), row),
        out_shape=jax.ShapeDtypeStruct((T, D_MODEL), F32),
        scratch_shapes=[pltpu.VMEM((N_MEM, 2 * D_MODEL), BF16)],
        compiler_params=pltpu.CompilerParams(dimension_semantics=("arbitrary",), vmem_limit_bytes=VMEM_LIMIT),
        name="post",
    )(x, d_out, g_out, mem, P["norm_mem_g"], P["w_xkv"], P["w_out"], P["norm_xattn_g"], P["w_xq"], P["w_xo"])


def _ffn_kernel(x_ref, prev_ref, next_ref, gn_ref, wup_ref, cw_ref, cb_ref, wdn_ref, gf_ref, o_ref, act_ref,
                *, tiles_per_seq):
    tm = x_ref.shape[0]
    j = pl.program_id(0) % tiles_per_seq
    has_prev = j > 0
    has_next = j < tiles_per_seq - 1
    x = x_ref[...]
    gn = gn_ref[...]
    rows = tm + 2 * HALO
    h_prev = jnp.where(has_prev, _rms(prev_ref[...], gn), 0.0)
    h_next = jnp.where(has_next, _rms(next_ref[...], gn), 0.0)
    h = jnp.concatenate([h_prev, _rms(x, gn), h_next], axis=0).astype(BF16)

    def conv_cols(start):
        u = jnp.dot(h, wup_ref[:, start:start + FF_CHUNK], preferred_element_type=F32)
        cw = cw_ref[:, start:start + FF_CHUNK]
        y = (pltpu.roll(u, 1, 0) * cw[0:1, :] + u * cw[1:2, :] + pltpu.roll(u, rows - 1, 0) * cw[2:3, :]
             + cb_ref[:, start:start + FF_CHUNK])
        return y[HALO:HALO + tm]

    for c in range(D_FF // FF_CHUNK):
        a = conv_cols(c * FF_CHUNK)
        b = conv_cols(D_FF + c * FF_CHUNK)
        act_ref[:, c * FF_CHUNK:(c + 1) * FF_CHUNK] = (a / (1.0 + jnp.exp(-a)) * b).astype(BF16)

    x3 = x + jnp.dot(act_ref[...], wdn_ref[...], preferred_element_type=F32)
    o_ref[...] = _rms(x3, gf_ref[...])


def _ffn(x, P, S):
    T = x.shape[0]
    tm = TOKEN_TILE
    halo_per_tile = tm // HALO
    last_halo = T // HALO - 1
    row = lambda i: (i, 0)
    const = lambda i: (0, 0)
    return pl.pallas_call(
        functools.partial(_ffn_kernel, tiles_per_seq=S // tm),
        grid=(T // tm,),
        in_specs=[pl.BlockSpec((tm, D_MODEL), row),
                  pl.BlockSpec((HALO, D_MODEL), lambda i: (jnp.maximum(i * halo_per_tile - 1, 0), 0)),
                  pl.BlockSpec((HALO, D_MODEL), lambda i: (jnp.minimum((i + 1) * halo_per_tile, last_halo), 0)),
                  pl.BlockSpec((1, D_MODEL), const),
                  pl.BlockSpec((D_MODEL, 2 * D_FF), const, pipeline_mode=pl.Buffered(1)),
                  pl.BlockSpec((3, 2 * D_FF), const), pl.BlockSpec((1, 2 * D_FF), const),
                  pl.BlockSpec((D_FF, D_MODEL), const, pipeline_mode=pl.Buffered(1)),
                  pl.BlockSpec((1, D_MODEL), const)],
        out_specs=pl.BlockSpec((tm, D_MODEL), row),
        out_shape=jax.ShapeDtypeStruct((T, D_MODEL), F32),
        scratch_shapes=[pltpu.VMEM((tm, D_FF), BF16)],
        compiler_params=pltpu.CompilerParams(dimension_semantics=("arbitrary",), vmem_limit_bytes=VMEM_LIMIT),
        name="ffn",
    )(x, x, x, P["norm_ffn_g"], P["w_up"], P["conv_w"], P["conv_b"], P["w_down"], P["final_norm_g"])


def _angles(p, dim, theta):
    inv = theta ** (-jnp.arange(0, dim, 2, dtype=F32) / dim)
    return p.astype(F32)[:, None] * inv[None, :]


def _rope_tables(S):
    pos = jnp.arange(S, dtype=jnp.int32)
    lane = jnp.arange(LANES, dtype=jnp.int32) % HEAD_DIM

    half = PARTIAL_ROPE_DIM // 2
    ang = _angles(pos, PARTIAL_ROPE_DIM, ROPE_THETA)[:, lane % half]
    rot = (lane < PARTIAL_ROPE_DIM)[None, :]
    first = ((lane % PARTIAL_ROPE_DIM) < half)[None, :]
    cp = jnp.where(rot, jnp.cos(ang), 1.0)
    sup = jnp.where(rot & first, -jnp.sin(ang), 0.0)
    sdp = jnp.where(rot & ~first, jnp.sin(ang), 0.0)

    half = HEAD_DIM // 4
    ang_r = _angles(pos // GRID_W, HEAD_DIM // 2, AXIAL_THETA)[:, lane % half]
    ang_c = _angles(pos % GRID_W, HEAD_DIM // 2, AXIAL_THETA)[:, lane % half]
    ang = jnp.where((lane < HEAD_DIM // 2)[None, :], ang_r, ang_c)
    first = ((lane % (HEAD_DIM // 2)) < half)[None, :]
    ca = jnp.cos(ang)
    sua = jnp.where(first, -jnp.sin(ang), 0.0)
    sda = jnp.where(first, 0.0, jnp.sin(ang))
    return tuple(t.astype(F32) for t in (cp, sup, sdp, ca, sua, sda))


def _rope_tables_t(S):
    pos = jnp.arange(S, dtype=jnp.int32)
    ang_p = _angles(pos, PARTIAL_ROPE_DIM, ROPE_THETA).T
    ang_r = _angles(pos // GRID_W, HEAD_DIM // 2, AXIAL_THETA).T
    ang_c = _angles(pos % GRID_W, HEAD_DIM // 2, AXIAL_THETA).T
    return (jnp.cos(ang_p), jnp.sin(ang_p), jnp.cos(ang_r), jnp.sin(ang_r), jnp.cos(ang_c), jnp.sin(ang_c))


def _trunk(x, mem, P):
    B, S, _ = x.shape
    xf = x.reshape(B * S, D_MODEL)
    dqt, dk, dvt, gqt, gk, gvt = _proj(xf, P, S)
    d_out = _attn(dqt, dk, dvt, P, B, S, diff=True)
    g_out = _attn(gqt, gk, gvt, P, B, S, diff=False)
    x2 = _post(xf, d_out, g_out, mem.reshape(B * N_MEM, D_MODEL), P, S)
    return _ffn(x2, P, S).reshape(B, S, D_MODEL)


def kernel(x_prompt, x_sample, mem_prompt, mem_sample, norm_mix_g, w_in, lambda_q1, lambda_k1, lambda_q2, lambda_k2, diff_subln_g, gqa_q_norm_g, gqa_k_norm_g, w_out, norm_xattn_g, norm_mem_g, w_xq, w_xkv, w_xo, norm_ffn_g, w_up, conv_w, conv_b, w_down, final_norm_g):
    assert w_in.shape[0] == 1, "single-layer trunk"
    s_max = max(x_prompt.shape[1], x_sample.shape[1])
    qb_max = max(ATTN_UNIT_ELEMS // x_prompt.shape[1], ATTN_UNIT_ELEMS // x_sample.shape[1])
    w = w_in[0].astype(BF16)
    q0, k0, v0, gq0, gk0, gv0, end = 0, 512, 1024, 1536, 2048, 2176, 2304
    w_in_k = jnp.concatenate([w[:, k0:v0], w[:, gk0:gv0]], axis=1)
    w_in_t = jnp.concatenate([w[:, q0:k0], w[:, v0:gq0], w[:, gq0:gk0], w[:, gv0:end]], axis=1).T
    P = {
        "norm_mix_g": norm_mix_g[0][None, :], "w_in_k": w_in_k, "w_in_t": w_in_t,
        "lambda_q1": lambda_q1, "lambda_k1": lambda_k1, "lambda_q2": lambda_q2, "lambda_k2": lambda_k2,
        "diff_subln_g_t": jnp.broadcast_to(diff_subln_g[0][:, None], (LANES, qb_max)),
        "gqa_q_norm_g_t": jnp.broadcast_to(gqa_q_norm_g[0][:, None], (HEAD_DIM, TOKEN_TILE)),
        "gqa_k_norm_g": jnp.concatenate([gqa_k_norm_g, gqa_k_norm_g], axis=-1),
        "w_out": w_out[0].astype(BF16), "norm_xattn_g": norm_xattn_g, "norm_mem_g": norm_mem_g,
        "w_xq": w_xq[0].astype(BF16), "w_xkv": w_xkv[0].astype(BF16), "w_xo": w_xo[0].astype(BF16),
        "norm_ffn_g": norm_ffn_g, "w_up": w_up[0].astype(BF16), "conv_w": conv_w[0], "conv_b": conv_b,
        "w_down": w_down[0].astype(BF16), "final_norm_g": final_norm_g[None, :],
        "tabs": _rope_tables(s_max), "tabs_t": _rope_tables_t(s_max),
    }
    y_prompt = _trunk(x_prompt, mem_prompt, P)
    y_sample = _trunk(x_sample, mem_sample, P)
    return (y_prompt, y_sample)
```

```python
import functools
import math

import jax
import jax.numpy as jnp
from jax import lax
from jax.experimental import pallas as pl
from jax.experimental.pallas import tpu as pltpu

F32 = jnp.float32
BF16 = jnp.bfloat16

D_MODEL = 1024
HEAD_DIM = 64
DIFF_HEADS = 4
GQA_HEADS = 8
GQA_KV_HEADS = 2
DIFF_W = 512
GQA_W = 512
KV_W = GQA_KV_HEADS * HEAD_DIM
ROPE_THETA = 500000.0
PARTIAL_ROPE_DIM = 16
AXIAL_THETA = 10000.0
GRID_W = 64
N_MEM = 256
XATTN_HEADS = 4
XATTN_HEAD_DIM = 256
D_FF = 2816
EPS = 1e-6
LAMBDA_INIT = 0.8 - 0.6 * math.exp(-0.3 * 0)
LOG2E = 1.4426950408889634

LANES = 128
SUBLANES = 8
HALO = SUBLANES
TOKEN_TILE = 512
FF_CHUNK = 256
ATTN_UNIT_ELEMS = 4096 * 256
ATTN_SCORE_CHUNK_ELEMS = 1024 * 256
ATTN_NARROW_QB = 256
N_PAIRS = 4
VMEM_LIMIT = 56 * 1024 * 1024

NT_DIMS = (((1,), (1,)), ((), ()))


def _rms(x, g):
    ms = jnp.mean(x * x, axis=-1, keepdims=True)
    return x * lax.rsqrt(ms + EPS) * g


def _rope(x, c, s_up, s_dn, shift):
    return x * c + pltpu.roll(x, LANES - shift, 1) * s_up + pltpu.roll(x, shift, 1) * s_dn


def _head_pair_rms(x, g, lo):
    sq = x * x
    tot = jnp.sum(sq, axis=-1, keepdims=True)
    first = jnp.sum(jnp.where(lo, sq, 0.0), axis=-1, keepdims=True)
    ss = jnp.where(lo, first, tot - first)
    return x * lax.rsqrt(ss * (1.0 / HEAD_DIM) + EPS) * g


def _rotate_rows(a, b, cos, sin):
    return a * cos - b * sin, a * sin + b * cos


def _proj_kernel(x_ref, g_ref, wk_ref, wt_ref, cp_ref, sup_ref, sdp_ref, ca_ref, sua_ref, sda_ref, gkn_ref,
                 cpt_ref, spt_ref, crt_ref, srt_ref, cct_ref, sct_ref, gqn_ref,
                 dqt_ref, dk_ref, dvt_ref, gqt_ref, gk_ref, gvt_ref):
    tm = x_ref.shape[0]
    h = _rms(x_ref[...], g_ref[...]).astype(BF16)
    qscale = HEAD_DIM ** -0.5 * LOG2E

    proj_k = jnp.dot(h, wk_ref[...], preferred_element_type=F32)
    proj_t = lax.dot_general(wt_ref[...], h, NT_DIMS, preferred_element_type=F32)

    def rows_t(start, n):
        return proj_t[start:start + n]

    cpt, spt = cpt_ref[...], spt_ref[...]
    half = PARTIAL_ROPE_DIM // 2
    for c in range(DIFF_HEADS):
        q = rows_t(c * LANES, LANES)
        parts = []
        for m in range(2):
            b0 = m * HEAD_DIM
            ra, rb = _rotate_rows(q[b0:b0 + half], q[b0 + half:b0 + 2 * half], cpt, spt)
            parts += [ra, rb, q[b0 + 2 * half:b0 + HEAD_DIM]]
        dqt_ref[c] = (jnp.concatenate(parts, axis=0) * qscale).astype(BF16)
        dvt_ref[c] = rows_t(DIFF_W + c * LANES, LANES).astype(BF16)

    crt, srt, cct, sct = crt_ref[...], srt_ref[...], cct_ref[...], sct_ref[...]
    gqn = gqn_ref[...]
    quarter = HEAD_DIM // 4
    for c in range(GQA_HEADS // 2):
        q = rows_t(2 * DIFF_W + c * LANES, LANES)
        parts = []
        for m in range(2):
            x = q[m * HEAD_DIM:(m + 1) * HEAD_DIM]
            ss = jnp.sum(x * x, axis=0, keepdims=True)
            x = x * lax.rsqrt(ss * (1.0 / HEAD_DIM) + EPS) * gqn
            parts += _rotate_rows(x[0:quarter], x[quarter:2 * quarter], crt, srt)
            parts += _rotate_rows(x[2 * quarter:3 * quarter], x[3 * quarter:HEAD_DIM], cct, sct)
        gqt_ref[c] = (jnp.concatenate(parts, axis=0) * qscale).astype(BF16)
    gvt_ref[...] = rows_t(2 * DIFF_W + GQA_W, KV_W).astype(BF16)

    lo = lax.broadcasted_iota(jnp.int32, (tm, LANES), 1) < HEAD_DIM
    cp, sup, sdp = cp_ref[...], sup_ref[...], sdp_ref[...]
    for c in range(DIFF_HEADS):
        k = proj_k[:, c * LANES:(c + 1) * LANES]
        dk_ref[c] = _rope(k, cp, sup, sdp, half).astype(BF16)
    k = _head_pair_rms(proj_k[:, DIFF_W:DIFF_W + KV_W], gkn_ref[...], lo)
    gk_ref[...] = _rope(k, ca_ref[...], sua_ref[...], sda_ref[...], quarter).astype(BF16)


def _proj(x, P, S):
    T = x.shape[0]
    tm = TOKEN_TILE
    tiles_per_seq = S // tm
    row = lambda i: (i, 0)
    const = lambda i: (0, 0)
    tab_spec = pl.BlockSpec((tm, LANES), lambda i: (i % tiles_per_seq, 0))
    tab_t = lambda n: pl.BlockSpec((n, tm), lambda i: (0, i % tiles_per_seq))
    vec = lambda n: pl.BlockSpec((1, n), const)
    head_t = lambda n, r: pl.BlockSpec((n, r, tm), lambda i: (0, 0, i))
    wk, wt = P["w_in_k"], P["w_in_t"]
    return pl.pallas_call(
        _proj_kernel,
        grid=(T // tm,),
        in_specs=[pl.BlockSpec((tm, D_MODEL), row), vec(D_MODEL),
                  pl.BlockSpec(wk.shape, const), pl.BlockSpec(wt.shape, const)]
                 + [tab_spec] * 6 + [vec(LANES)]
                 + [tab_t(8), tab_t(8), tab_t(16), tab_t(16), tab_t(16), tab_t(16)]
                 + [pl.BlockSpec((HEAD_DIM, tm), const)],
        out_specs=[head_t(DIFF_HEADS, LANES), pl.BlockSpec((DIFF_HEADS, tm, LANES), lambda i: (0, i, 0)),
                   head_t(DIFF_HEADS, LANES), head_t(GQA_HEADS // 2, LANES),
                   pl.BlockSpec((tm, LANES), row), pl.BlockSpec((KV_W, tm), lambda i: (0, i))],
        out_shape=[jax.ShapeDtypeStruct((DIFF_HEADS, LANES, T), BF16),
                   jax.ShapeDtypeStruct((DIFF_HEADS, T, LANES), BF16),
                   jax.ShapeDtypeStruct((DIFF_HEADS, LANES, T), BF16),
                   jax.ShapeDtypeStruct((GQA_HEADS // 2, LANES, T), BF16),
                   jax.ShapeDtypeStruct((T, LANES), BF16),
                   jax.ShapeDtypeStruct((KV_W, T), BF16)],
        compiler_params=pltpu.CompilerParams(dimension_semantics=("arbitrary",), vmem_limit_bytes=VMEM_LIMIT),
        name="proj",
    )(x, P["norm_mix_g"], wk, wt, *P["tabs"], P["gqa_k_norm_g"], *P["tabs_t"], P["gqa_q_norm_g_t"])


def _pad_queries(q_rows, hi):
    z = jnp.zeros_like(q_rows)
    return jnp.concatenate([z, q_rows] if hi else [q_rows, z], axis=0)


def _column_max(chunks):
    while len(chunks) > 1:
        chunks = [jnp.maximum(a, b) for a, b in zip(chunks[0::2], chunks[1::2])]
    return jnp.max(chunks[0], axis=0, keepdims=True)


def _score_chunk(k_rows, qpad, s_ref, c, rows, acc):
    s = jnp.dot(k_rows(c, rows), qpad, preferred_element_type=F32)
    s_ref[c:c + rows, :] = s
    half = rows // 2
    for r in (0, half):
        acc.append(jnp.max(s[r:r + half].reshape(half // SUBLANES, SUBLANES, s.shape[1]), axis=0))


def _score_unit(q_rows, hi, k_rows, s_ref, m_ref):
    n, qb = s_ref.shape
    rows = ATTN_SCORE_CHUNK_ELEMS // qb
    qpad = _pad_queries(q_rows, hi)
    acc = []
    for c in range(0, n, rows):
        _score_chunk(k_rows, qpad, s_ref, c, rows, acc)
    m_ref[...] = _column_max(acc)


def _phase(q_rows, hi, k_rows, s_nxt, m_nxt, s_cur, m_cur, vt_cols):
    n, qb = s_cur.shape
    a_rows = ATTN_SCORE_CHUNK_ELEMS // qb
    b_rows = a_rows // 2
    interleave = qb > ATTN_NARROW_QB
    qpad = _pad_queries(q_rows, hi)
    m = m_cur[...]
    acc, sums, pending = [], [], []
    o = None

    def value_matmuls(o):
        for c0, p in pending:
            oc = jnp.dot(vt_cols(c0, b_rows), p, preferred_element_type=F32)
            o = oc if o is None else o + oc
        return o

    def softmax_chunk(r):
        p = jnp.exp2(s_cur[r:r + b_rows, :] - m)
        sums.append(jnp.sum(p.reshape(b_rows // SUBLANES, SUBLANES, qb), axis=0))
        return r, p.astype(BF16)

    for c in range(0, n, a_rows):
        _score_chunk(k_rows, qpad, s_nxt, c, a_rows, acc)
        if interleave:
            o = value_matmuls(o)
            pending = [softmax_chunk(r) for r in (c, c + b_rows)]
    if not interleave:
        for r in range(0, n, b_rows):
            pending = [softmax_chunk(r)]
            o = value_matmuls(o)
    else:
        o = value_matmuls(o)
    m_nxt[...] = _column_max(acc)
    while len(sums) > 1:
        sums = [a + b for a, b in zip(sums[0::2], sums[1::2])]
    return o / jnp.sum(sums[0], axis=0, keepdims=True)


def _attn_kernel(*refs, diff):
    if diff:
        (qt_ref, qtn_ref, k_ref, kn_ref, vt_ref, lq1_ref, lk1_ref, lq2_ref, lk2_ref, sg_ref,
         o_ref, s0, s1, m0, m1) = refs
        lam = (jnp.exp(jnp.sum(lq1_ref[...] * lk1_ref[...], axis=-1, keepdims=True))
               - jnp.exp(jnp.sum(lq2_ref[...] * lk2_ref[...], axis=-1, keepdims=True)) + LAMBDA_INIT)
    else:
        qt_ref, qtn_ref, k_ref, kn_ref, vt_ref, o_ref, s0, s1, m0, m1 = refs

    def keys(j):
        return (lambda c, r: k_ref[j, c:c + r, :]) if diff else (lambda c, r: k_ref[c:c + r, :])

    def next_keys(c, r):
        return kn_ref[0, c:c + r, :] if diff else kn_ref[c:c + r, :]

    def values(j):
        head = j if diff else j // 2
        return lambda c, r: vt_ref[head, :, c:c + r]

    def his(j):
        return (False, True) if diff else (j // 2 == 1, j // 2 == 1)

    @pl.when(jnp.logical_and(pl.program_id(0) == 0, pl.program_id(1) == 0))
    def _():
        _score_unit(qt_ref[0, 0:HEAD_DIM, :], False, keys(0), s0, m0)

    for j in range(N_PAIRS):
        oa = _phase(qt_ref[j, HEAD_DIM:2 * HEAD_DIM, :], his(j)[1], keys(j), s1, m1, s0, m0, values(j))
        if j + 1 < N_PAIRS:
            nxt = (qt_ref[j + 1, 0:HEAD_DIM, :], his(j + 1)[0], keys(j + 1))
        else:
            nxt = (qtn_ref[0, 0:HEAD_DIM, :], False, next_keys)
        ob = _phase(*nxt, s0, m0, s1, m1, values(j))
        if diff:
            d = oa - lam * ob
            ms = jnp.mean(d * d, axis=0, keepdims=True)
            out = d * lax.rsqrt(ms + EPS) * sg_ref[...] * (1.0 - LAMBDA_INIT)
        else:
            out = jnp.concatenate([oa, ob], axis=0)
        o_ref[j] = out.T.astype(BF16)


def _attn(qt, k, vt, P, B, S, diff):
    T = qt.shape[-1]
    qb = ATTN_UNIT_ELEMS // S
    nq = S // qb
    last = B * nq - 1
    cur = lambda b, i: b * nq + i
    nxt = lambda b, i: jnp.minimum(b * nq + i + 1, last)
    const = lambda b, i: (0, 0)
    if diff:
        k_specs = [pl.BlockSpec((DIFF_HEADS, S, LANES), lambda b, i: (0, b, 0)),
                   pl.BlockSpec((1, S, LANES), lambda b, i: (0, nxt(b, i) // nq, 0))]
        vt_spec = pl.BlockSpec((DIFF_HEADS, LANES, S), lambda b, i: (0, 0, b))
        extra = [P["lambda_q1"], P["lambda_k1"], P["lambda_q2"], P["lambda_k2"], P["diff_subln_g_t"]]
        extra_specs = [pl.BlockSpec((1, HEAD_DIM), const)] * 4 + [pl.BlockSpec((LANES, qb), const)]
        name = "diff_attn"
    else:
        k_specs = [pl.BlockSpec((S, LANES), lambda b, i: (b, 0)),
                   pl.BlockSpec((S, LANES), lambda b, i: (nxt(b, i) // nq, 0))]
        vt = vt.reshape(GQA_KV_HEADS, HEAD_DIM, T)
        vt_spec = pl.BlockSpec((GQA_KV_HEADS, HEAD_DIM, S), lambda b, i: (0, 0, b))
        extra, extra_specs = [], []
        name = "gqa_attn"
    return pl.pallas_call(
        functools.partial(_attn_kernel, diff=diff),
        grid=(B, nq),
        in_specs=[pl.BlockSpec((N_PAIRS, LANES, qb), lambda b, i: (0, 0, cur(b, i))),
                  pl.BlockSpec((1, LANES, qb), lambda b, i: (0, 0, nxt(b, i)))] + k_specs + [vt_spec] + extra_specs,
        out_specs=pl.BlockSpec((N_PAIRS, qb, LANES), lambda b, i: (0, cur(b, i), 0)),
        out_shape=jax.ShapeDtypeStruct((N_PAIRS, T, LANES), BF16),
        scratch_shapes=[pltpu.VMEM((S, qb), F32), pltpu.VMEM((S, qb), F32),
                        pltpu.VMEM((1, qb), F32), pltpu.VMEM((1, qb), F32)],
        compiler_params=pltpu.CompilerParams(dimension_semantics=("arbitrary", "arbitrary"),
                                             vmem_limit_bytes=VMEM_LIMIT),
        name=name,
    )(qt, qt, k, k, vt, *extra)


def _post_kernel(x_ref, d_ref, g_ref, mem_ref, gm_ref, wkv_ref, wout_ref, gx_ref, wxq_ref, wxo_ref, o_ref, kv_ref,
                 *, tiles_per_seq):
    @pl.when(pl.program_id(0) % tiles_per_seq == 0)
    def _():
        m = _rms(mem_ref[...], gm_ref[...]).astype(BF16)
        kv_ref[...] = jnp.dot(m, wkv_ref[...], preferred_element_type=F32).astype(BF16)

    mix = jnp.concatenate([d_ref[c] for c in range(N_PAIRS)] + [g_ref[c] for c in range(N_PAIRS)], axis=-1)
    x1 = x_ref[...] + jnp.dot(mix, wout_ref[...], preferred_element_type=F32)
    h = _rms(x1, gx_ref[...]).astype(BF16)
    q = (jnp.dot(h, wxq_ref[...], preferred_element_type=F32) * (XATTN_HEAD_DIM ** -0.5 * LOG2E)).astype(BF16)
    outs = []
    for hh in range(XATTN_HEADS):
        sl = slice(hh * XATTN_HEAD_DIM, (hh + 1) * XATTN_HEAD_DIM)
        vsl = slice(D_MODEL + hh * XATTN_HEAD_DIM, D_MODEL + (hh + 1) * XATTN_HEAD_DIM)
        s = lax.dot_general(q[:, sl], kv_ref[:, sl], NT_DIMS, preferred_element_type=F32)
        m = jnp.max(s, axis=-1, keepdims=True)
        p = jnp.exp2(s - m)
        l = jnp.sum(p, axis=-1, keepdims=True)
        o = jnp.dot(p.astype(BF16), kv_ref[:, vsl], preferred_element_type=F32) / l
        outs.append(o.astype(BF16))
    o = jnp.concatenate(outs, axis=-1)
    o_ref[...] = x1 + jnp.dot(o, wxo_ref[...], preferred_element_type=F32)


def _post(x, d_out, g_out, mem, P, S):
    T = x.shape[0]
    tm = TOKEN_TILE
    tiles_per_seq = S // tm
    row = lambda i: (i, 0)
    const = lambda i: (0, 0)
    heads = pl.BlockSpec((N_PAIRS, tm, LANES), lambda i: (0, i, 0))
    return pl.pallas_call(
        functools.partial(_post_kernel, tiles_per_seq=tiles_per_seq),
        grid=(T // tm,),
        in_specs=[pl.BlockSpec((tm, D_MODEL), row), heads, heads,
                  pl.BlockSpec((N_MEM, D_MODEL), lambda i: (i // tiles_per_seq, 0)),
                  pl.BlockSpec((1, D_MODEL), const), pl.BlockSpec((D_MODEL, 2 * D_MODEL), const),
                  pl.BlockSpec((D_MODEL, D_MODEL), const), pl.BlockSpec((1, D_MODEL), const),
                  pl.BlockSpec((D_MODEL, D_MODEL), const), pl.BlockSpec((D_MODEL, D_MODEL), const)],
        out_specs=pl.BlockSpec((tm, D_MODEL), row),
        out_shape=jax.ShapeDtypeStruct((T, D_MODEL), F32),
        scratch_shapes=[pltpu.VMEM((N_MEM, 2 * D_MODEL), BF16)],
        compiler_params=pltpu.CompilerParams(dimension_semantics=("arbitrary",), vmem_limit_bytes=VMEM_LIMIT),
        name="post",
    )(x, d_out, g_out, mem, P["norm_mem_g"], P["w_xkv"], P["w_out"], P["norm_xattn_g"], P["w_xq"], P["w_xo"])


def _ffn_kernel(x_ref, prev_ref, next_ref, gn_ref, wup_ref, cw_ref, cb_ref, wdn_ref, gf_ref, o_ref, act_ref,
                *, tiles_per_seq):
    tm = x_ref.shape[0]
    j = pl.program_id(0) % tiles_per_seq
    has_prev = j > 0
    has_next = j < tiles_per_seq - 1
    x = x_ref[...]
    gn = gn_ref[...]
    rows = tm + 2 * HALO
    h_prev = jnp.where(has_prev, _rms(prev_ref[...], gn), 0.0)
    h_next = jnp.where(has_next, _rms(next_ref[...], gn), 0.0)
    h = jnp.concatenate([h_prev, _rms(x, gn), h_next], axis=0).astype(BF16)

    def conv_cols(start):
        u = jnp.dot(h, wup_ref[:, start:start + FF_CHUNK], preferred_element_type=F32)
        cw = cw_ref[:, start:start + FF_CHUNK]
        y = (pltpu.roll(u, 1, 0) * cw[0:1, :] + u * cw[1:2, :] + pltpu.roll(u, rows - 1, 0) * cw[2:3, :]
             + cb_ref[:, start:start + FF_CHUNK])
        return y[HALO:HALO + tm]

    for c in range(D_FF // FF_CHUNK):
        a = conv_cols(c * FF_CHUNK)
        b = conv_cols(D_FF + c * FF_CHUNK)
        act_ref[:, c * FF_CHUNK:(c + 1) * FF_CHUNK] = (a / (1.0 + jnp.exp(-a)) * b).astype(BF16)

    gf = gf_ref[...]
    for r in range(0, tm, tm // 2):
        rs = slice(r, r + tm // 2)
        x3 = x[rs] + jnp.dot(act_ref[rs, :], wdn_ref[...], preferred_element_type=F32)
        o_ref[rs, :] = _rms(x3, gf)


def _ffn(x, P, S):
    T = x.shape[0]
    tm = TOKEN_TILE
    halo_per_tile = tm // HALO
    last_halo = T // HALO - 1
    row = lambda i: (i, 0)
    const = lambda i: (0, 0)
    return pl.pallas_call(
        functools.partial(_ffn_kernel, tiles_per_seq=S // tm),
        grid=(T // tm,),
        in_specs=[pl.BlockSpec((tm, D_MODEL), row),
                  pl.BlockSpec((HALO, D_MODEL), lambda i: (jnp.maximum(i * halo_per_tile - 1, 0), 0)),
                  pl.BlockSpec((HALO, D_MODEL), lambda i: (jnp.minimum((i + 1) * halo_per_tile, last_halo), 0)),
                  pl.BlockSpec((1, D_MODEL), const),
                  pl.BlockSpec((D_MODEL, 2 * D_FF), const, pipeline_mode=pl.Buffered(1)),
                  pl.BlockSpec((3, 2 * D_FF), const), pl.BlockSpec((1, 2 * D_FF), const),
                  pl.BlockSpec((D_FF, D_MODEL), const, pipeline_mode=pl.Buffered(1)),
                  pl.BlockSpec((1, D_MODEL), const)],
        out_specs=pl.BlockSpec((tm, D_MODEL), row),
        out_shape=jax.ShapeDtypeStruct((T, D_MODEL), F32),
        scratch_shapes=[pltpu.VMEM((tm, D_FF), BF16)],
        compiler_params=pltpu.CompilerParams(dimension_semantics=("arbitrary",), vmem_limit_bytes=VMEM_LIMIT),
        name="ffn",
    )(x, x, x, P["norm_ffn_g"], P["w_up"], P["conv_w"], P["conv_b"], P["w_down"], P["final_norm_g"])


def _angles(p, dim, theta):
    inv = theta ** (-jnp.arange(0, dim, 2, dtype=F32) / dim)
    return p.astype(F32)[:, None] * inv[None, :]


def _rope_tables(S):
    pos = jnp.arange(S, dtype=jnp.int32)
    ang_p = _angles(pos, PARTIAL_ROPE_DIM, ROPE_THETA)
    ang_r = _angles(pos // GRID_W, HEAD_DIM // 2, AXIAL_THETA)
    ang_c = _angles(pos % GRID_W, HEAD_DIM // 2, AXIAL_THETA)
    small = tuple(f(a) for a in (ang_p, ang_r, ang_c) for f in (jnp.cos, jnp.sin))
    cos_p, sin_p, cos_r, sin_r, cos_c, sin_c = small
    lane = jnp.arange(LANES, dtype=jnp.int32) % HEAD_DIM

    half = PARTIAL_ROPE_DIM // 2
    idx = lane % half
    rot = (lane < PARTIAL_ROPE_DIM)[None, :]
    first = ((lane % PARTIAL_ROPE_DIM) < half)[None, :]
    cp = jnp.where(rot, cos_p[:, idx], 1.0)
    sup = jnp.where(rot & first, -sin_p[:, idx], 0.0)
    sdp = jnp.where(rot & ~first, sin_p[:, idx], 0.0)

    half = HEAD_DIM // 4
    idx = lane % half
    by_row = (lane < HEAD_DIM // 2)[None, :]
    first = ((lane % (HEAD_DIM // 2)) < half)[None, :]
    sin_a = jnp.where(by_row, sin_r[:, idx], sin_c[:, idx])
    ca = jnp.where(by_row, cos_r[:, idx], cos_c[:, idx])
    sua = jnp.where(first, -sin_a, 0.0)
    sda = jnp.where(first, 0.0, sin_a)
    return (cp, sup, sdp, ca, sua, sda), tuple(t.T for t in small)


def _trunk(x, mem, P):
    B, S, _ = x.shape
    xf = x.reshape(B * S, D_MODEL)
    dqt, dk, dvt, gqt, gk, gvt = _proj(xf, P, S)
    d_out = _attn(dqt, dk, dvt, P, B, S, diff=True)
    g_out = _attn(gqt, gk, gvt, P, B, S, diff=False)
    x2 = _post(xf, d_out, g_out, mem.reshape(B * N_MEM, D_MODEL), P, S)
    return _ffn(x2, P, S).reshape(B, S, D_MODEL)


def kernel(x_prompt, x_sample, mem_prompt, mem_sample, norm_mix_g, w_in, lambda_q1, lambda_k1, lambda_q2, lambda_k2, diff_subln_g, gqa_q_norm_g, gqa_k_norm_g, w_out, norm_xattn_g, norm_mem_g, w_xq, w_xkv, w_xo, norm_ffn_g, w_up, conv_w, conv_b, w_down, final_norm_g):
    assert w_in.shape[0] == 1, "single-layer trunk"
    s_max = max(x_prompt.shape[1], x_sample.shape[1])
    qb_max = max(ATTN_UNIT_ELEMS // x_prompt.shape[1], ATTN_UNIT_ELEMS // x_sample.shape[1])
    tabs, tabs_t = _rope_tables(s_max)
    w = w_in[0].astype(BF16)
    q0, k0, v0, gq0, gk0, gv0, end = 0, 512, 1024, 1536, 2048, 2176, 2304
    w_in_k = jnp.concatenate([w[:, k0:v0], w[:, gk0:gv0]], axis=1)
    w_in_t = jnp.concatenate([w[:, q0:k0], w[:, v0:gq0], w[:, gq0:gk0], w[:, gv0:end]], axis=1).T
    P = {
        "norm_mix_g": norm_mix_g[0][None, :], "w_in_k": w_in_k, "w_in_t": w_in_t,
        "lambda_q1": lambda_q1, "lambda_k1": lambda_k1, "lambda_q2": lambda_q2, "lambda_k2": lambda_k2,
        "diff_subln_g_t": jnp.broadcast_to(diff_subln_g[0][:, None], (LANES, qb_max)),
        "gqa_q_norm_g_t": jnp.broadcast_to(gqa_q_norm_g[0][:, None], (HEAD_DIM, TOKEN_TILE)),
        "gqa_k_norm_g": jnp.concatenate([gqa_k_norm_g, gqa_k_norm_g], axis=-1),
        "w_out": w_out[0].astype(BF16), "norm_xattn_g": norm_xattn_g, "norm_mem_g": norm_mem_g,
        "w_xq": w_xq[0].astype(BF16), "w_xkv": w_xkv[0].astype(BF16), "w_xo": w_xo[0].astype(BF16),
        "norm_ffn_g": norm_ffn_g, "w_up": w_up[0].astype(BF16), "conv_w": conv_w[0], "conv_b": conv_b,
        "w_down": w_down[0].astype(BF16), "final_norm_g": final_norm_g[None, :],
        "tabs": tabs, "tabs_t": tabs_t,
    }
    y_prompt = _trunk(x_prompt, mem_prompt, P)
    y_sample = _trunk(x_sample, mem_sample, P)
    return (y_prompt, y_sample)
```

```python
import functools
import math

import jax
import jax.numpy as jnp
import numpy as np
from jax import lax
from jax.experimental import pallas as pl
from jax.experimental.pallas import tpu as pltpu

F32 = jnp.float32
BF16 = jnp.bfloat16

D_MODEL = 1024
HEAD_DIM = 64
DIFF_HEADS = 4
GQA_HEADS = 8
GQA_KV_HEADS = 2
DIFF_W = 512
GQA_W = 512
KV_W = GQA_KV_HEADS * HEAD_DIM
ROPE_THETA = 500000.0
PARTIAL_ROPE_DIM = 16
AXIAL_THETA = 10000.0
GRID_W = 64
N_MEM = 256
XATTN_HEADS = 4
XATTN_HEAD_DIM = 256
D_FF = 2816
EPS = 1e-6
LAMBDA_INIT = 0.8 - 0.6 * math.exp(-0.3 * 0)
LOG2E = 1.4426950408889634

LANES = 128
SUBLANES = 8
HALO = SUBLANES
TOKEN_TILE = 512
FF_CHUNK = 256
ATTN_UNIT_ELEMS = 4096 * 256
ATTN_SCORE_CHUNK_ELEMS = 1024 * 256
ATTN_NARROW_QB = 256
N_PAIRS = 4
VMEM_LIMIT = 56 * 1024 * 1024

NT_DIMS = (((1,), (1,)), ((), ()))


def _rms(x, g):
    ms = jnp.mean(x * x, axis=-1, keepdims=True)
    return x * lax.rsqrt(ms + EPS) * g


def _rope(x, c, s_up, s_dn, shift):
    return x * c + pltpu.roll(x, LANES - shift, 1) * s_up + pltpu.roll(x, shift, 1) * s_dn


def _head_pair_rms(x, g, lo):
    sq = x * x
    tot = jnp.sum(sq, axis=-1, keepdims=True)
    first = jnp.sum(jnp.where(lo, sq, 0.0), axis=-1, keepdims=True)
    ss = jnp.where(lo, first, tot - first)
    return x * lax.rsqrt(ss * (1.0 / HEAD_DIM) + EPS) * g


def _rotate_rows(a, b, cos, sin):
    return a * cos - b * sin, a * sin + b * cos


def _proj_kernel(x_ref, g_ref, wk_ref, wt_ref, cp_ref, sup_ref, sdp_ref, ca_ref, sua_ref, sda_ref, gkn_ref,
                 cpt_ref, spt_ref, crt_ref, srt_ref, cct_ref, sct_ref, gqn_ref,
                 dqt_ref, dk_ref, dvt_ref, gqt_ref, gk_ref, gvt_ref):
    tm = x_ref.shape[0]
    h = _rms(x_ref[...], g_ref[...]).astype(BF16)
    qscale = HEAD_DIM ** -0.5 * LOG2E

    proj_k = jnp.dot(h, wk_ref[...], preferred_element_type=F32)
    proj_t = lax.dot_general(wt_ref[...], h, NT_DIMS, preferred_element_type=F32)

    def rows_t(start, n):
        return proj_t[start:start + n]

    cpt, spt = cpt_ref[...], spt_ref[...]
    half = PARTIAL_ROPE_DIM // 2
    for c in range(DIFF_HEADS):
        q = rows_t(c * LANES, LANES)
        parts = []
        for m in range(2):
            b0 = m * HEAD_DIM
            ra, rb = _rotate_rows(q[b0:b0 + half], q[b0 + half:b0 + 2 * half], cpt, spt)
            parts += [ra, rb, q[b0 + 2 * half:b0 + HEAD_DIM]]
        dqt_ref[c] = (jnp.concatenate(parts, axis=0) * qscale).astype(BF16)
        dvt_ref[c] = rows_t(DIFF_W + c * LANES, LANES).astype(BF16)

    crt, srt, cct, sct = crt_ref[...], srt_ref[...], cct_ref[...], sct_ref[...]
    gqn = gqn_ref[...]
    quarter = HEAD_DIM // 4
    for c in range(GQA_HEADS // 2):
        q = rows_t(2 * DIFF_W + c * LANES, LANES)
        parts = []
        for m in range(2):
            x = q[m * HEAD_DIM:(m + 1) * HEAD_DIM]
            ss = jnp.sum(x * x, axis=0, keepdims=True)
            x = x * lax.rsqrt(ss * (1.0 / HEAD_DIM) + EPS) * gqn
            parts += _rotate_rows(x[0:quarter], x[quarter:2 * quarter], crt, srt)
            parts += _rotate_rows(x[2 * quarter:3 * quarter], x[3 * quarter:HEAD_DIM], cct, sct)
        gqt_ref[c] = (jnp.concatenate(parts, axis=0) * qscale).astype(BF16)
    gvt_ref[...] = rows_t(2 * DIFF_W + GQA_W, KV_W).astype(BF16)

    lo = lax.broadcasted_iota(jnp.int32, (tm, LANES), 1) < HEAD_DIM
    cp, sup, sdp = cp_ref[...], sup_ref[...], sdp_ref[...]
    for c in range(DIFF_HEADS):
        k = proj_k[:, c * LANES:(c + 1) * LANES]
        dk_ref[c] = _rope(k, cp, sup, sdp, half).astype(BF16)
    k = _head_pair_rms(proj_k[:, DIFF_W:DIFF_W + KV_W], gkn_ref[...], lo)
    gk_ref[...] = _rope(k, ca_ref[...], sua_ref[...], sda_ref[...], quarter).astype(BF16)


def _proj(x, P, S):
    T = x.shape[0]
    tm = TOKEN_TILE
    tiles_per_seq = S // tm
    row = lambda i: (i, 0)
    const = lambda i: (0, 0)
    tab_spec = pl.BlockSpec((tm, LANES), lambda i: (i % tiles_per_seq, 0))
    tab_t = lambda n: pl.BlockSpec((n, tm), lambda i: (0, i % tiles_per_seq))
    vec = lambda n: pl.BlockSpec((1, n), const)
    head_t = lambda n, r: pl.BlockSpec((n, r, tm), lambda i: (0, 0, i))
    wk, wt = P["w_in_k"], P["w_in_t"]
    return pl.pallas_call(
        _proj_kernel,
        grid=(T // tm,),
        in_specs=[pl.BlockSpec((tm, D_MODEL), row), vec(D_MODEL),
                  pl.BlockSpec(wk.shape, const), pl.BlockSpec(wt.shape, const)]
                 + [tab_spec] * 6 + [vec(LANES)]
                 + [tab_t(PARTIAL_ROPE_DIM // 2)] * 2 + [tab_t(HEAD_DIM // 4)] * 4
                 + [pl.BlockSpec((HEAD_DIM, tm), const)],
        out_specs=[head_t(DIFF_HEADS, LANES), pl.BlockSpec((DIFF_HEADS, tm, LANES), lambda i: (0, i, 0)),
                   head_t(DIFF_HEADS, LANES), head_t(GQA_HEADS // 2, LANES),
                   pl.BlockSpec((tm, LANES), row), pl.BlockSpec((KV_W, tm), lambda i: (0, i))],
        out_shape=[jax.ShapeDtypeStruct((DIFF_HEADS, LANES, T), BF16),
                   jax.ShapeDtypeStruct((DIFF_HEADS, T, LANES), BF16),
                   jax.ShapeDtypeStruct((DIFF_HEADS, LANES, T), BF16),
                   jax.ShapeDtypeStruct((GQA_HEADS // 2, LANES, T), BF16),
                   jax.ShapeDtypeStruct((T, LANES), BF16),
                   jax.ShapeDtypeStruct((KV_W, T), BF16)],
        compiler_params=pltpu.CompilerParams(dimension_semantics=("arbitrary",), vmem_limit_bytes=VMEM_LIMIT),
        name="proj",
    )(x, P["norm_mix_g"], wk, wt, *P["tabs"], P["gqa_k_norm_g"], *P["tabs_t"], P["gqa_q_norm_g_t"])


def _pad_queries(q_rows, hi):
    z = jnp.zeros_like(q_rows)
    return jnp.concatenate([z, q_rows] if hi else [q_rows, z], axis=0)


def _column_max(chunks):
    while len(chunks) > 1:
        chunks = [jnp.maximum(a, b) for a, b in zip(chunks[0::2], chunks[1::2])]
    return jnp.max(chunks[0], axis=0, keepdims=True)


def _score_chunk(k_rows, qpad, s_ref, c, rows, acc):
    s = jnp.dot(k_rows(c, rows), qpad, preferred_element_type=F32)
    s_ref[c:c + rows, :] = s
    half = rows // 2
    for r in (0, half):
        acc.append(jnp.max(s[r:r + half].reshape(half // SUBLANES, SUBLANES, s.shape[1]), axis=0))


def _score_unit(q_rows, hi, k_rows, s_ref, m_ref):
    n, qb = s_ref.shape
    rows = ATTN_SCORE_CHUNK_ELEMS // qb
    qpad = _pad_queries(q_rows, hi)
    acc = []
    for c in range(0, n, rows):
        _score_chunk(k_rows, qpad, s_ref, c, rows, acc)
    m_ref[...] = _column_max(acc)


def _phase(q_rows, hi, k_rows, s_nxt, m_nxt, s_cur, m_cur, vt_cols):
    n, qb = s_cur.shape
    a_rows = ATTN_SCORE_CHUNK_ELEMS // qb
    b_rows = a_rows // 2
    interleave = qb > ATTN_NARROW_QB
    qpad = _pad_queries(q_rows, hi)
    m = m_cur[...]
    acc, sums, pending = [], [], []
    o = None

    def value_matmuls(o):
        for c0, p in pending:
            oc = jnp.dot(vt_cols(c0, b_rows), p, preferred_element_type=F32)
            o = oc if o is None else o + oc
        return o

    def softmax_chunk(r):
        p = jnp.exp2(s_cur[r:r + b_rows, :] - m)
        sums.append(jnp.sum(p.reshape(b_rows // SUBLANES, SUBLANES, qb), axis=0))
        return r, p.astype(BF16)

    for c in range(0, n, a_rows):
        _score_chunk(k_rows, qpad, s_nxt, c, a_rows, acc)
        if interleave:
            o = value_matmuls(o)
            pending = [softmax_chunk(r) for r in (c, c + b_rows)]
    if not interleave:
        for r in range(0, n, b_rows):
            pending = [softmax_chunk(r)]
            o = value_matmuls(o)
    else:
        o = value_matmuls(o)
    m_nxt[...] = _column_max(acc)
    while len(sums) > 1:
        sums = [a + b for a, b in zip(sums[0::2], sums[1::2])]
    return o / jnp.sum(sums[0], axis=0, keepdims=True)


def _attn_kernel(*refs, diff):
    if diff:
        (qt_ref, qtn_ref, k_ref, kn_ref, vt_ref, lq1_ref, lk1_ref, lq2_ref, lk2_ref, sg_ref,
         o_ref, s0, s1, m0, m1) = refs
        lam = (jnp.exp(jnp.sum(lq1_ref[...] * lk1_ref[...], axis=-1, keepdims=True))
               - jnp.exp(jnp.sum(lq2_ref[...] * lk2_ref[...], axis=-1, keepdims=True)) + LAMBDA_INIT)
    else:
        qt_ref, qtn_ref, k_ref, kn_ref, vt_ref, o_ref, s0, s1, m0, m1 = refs

    def keys(j):
        return (lambda c, r: k_ref[j, c:c + r, :]) if diff else (lambda c, r: k_ref[c:c + r, :])

    def next_keys(c, r):
        return kn_ref[0, c:c + r, :] if diff else kn_ref[c:c + r, :]

    def values(j):
        head = j if diff else j // 2
        return lambda c, r: vt_ref[head, :, c:c + r]

    def his(j):
        return (False, True) if diff else (j // 2 == 1, j // 2 == 1)

    @pl.when(jnp.logical_and(pl.program_id(0) == 0, pl.program_id(1) == 0))
    def _():
        _score_unit(qt_ref[0, 0:HEAD_DIM, :], False, keys(0), s0, m0)

    for j in range(N_PAIRS):
        oa = _phase(qt_ref[j, HEAD_DIM:2 * HEAD_DIM, :], his(j)[1], keys(j), s1, m1, s0, m0, values(j))
        if j + 1 < N_PAIRS:
            nxt = (qt_ref[j + 1, 0:HEAD_DIM, :], his(j + 1)[0], keys(j + 1))
        else:
            nxt = (qtn_ref[0, 0:HEAD_DIM, :], False, next_keys)
        ob = _phase(*nxt, s0, m0, s1, m1, values(j))
        if diff:
            d = oa - lam * ob
            ms = jnp.mean(d * d, axis=0, keepdims=True)
            out = d * lax.rsqrt(ms + EPS) * sg_ref[...] * (1.0 - LAMBDA_INIT)
        else:
            out = jnp.concatenate([oa, ob], axis=0)
        o_ref[j] = out.T.astype(BF16)


def _attn(qt, k, vt, P, B, S, diff):
    T = qt.shape[-1]
    qb = ATTN_UNIT_ELEMS // S
    nq = S // qb
    last = B * nq - 1
    cur = lambda b, i: b * nq + i
    nxt = lambda b, i: jnp.minimum(b * nq + i + 1, last)
    const = lambda b, i: (0, 0)
    if diff:
        k_specs = [pl.BlockSpec((DIFF_HEADS, S, LANES), lambda b, i: (0, b, 0)),
                   pl.BlockSpec((1, S, LANES), lambda b, i: (0, nxt(b, i) // nq, 0))]
        vt_spec = pl.BlockSpec((DIFF_HEADS, LANES, S), lambda b, i: (0, 0, b))
        extra = [P["lambda_q1"], P["lambda_k1"], P["lambda_q2"], P["lambda_k2"], P["diff_subln_g_t"]]
        extra_specs = [pl.BlockSpec((1, HEAD_DIM), const)] * 4 + [pl.BlockSpec((LANES, qb), const)]
        name = "diff_attn"
    else:
        k_specs = [pl.BlockSpec((S, LANES), lambda b, i: (b, 0)),
                   pl.BlockSpec((S, LANES), lambda b, i: (nxt(b, i) // nq, 0))]
        vt = vt.reshape(GQA_KV_HEADS, HEAD_DIM, T)
        vt_spec = pl.BlockSpec((GQA_KV_HEADS, HEAD_DIM, S), lambda b, i: (0, 0, b))
        extra, extra_specs = [], []
        name = "gqa_attn"
    return pl.pallas_call(
        functools.partial(_attn_kernel, diff=diff),
        grid=(B, nq),
        in_specs=[pl.BlockSpec((N_PAIRS, LANES, qb), lambda b, i: (0, 0, cur(b, i))),
                  pl.BlockSpec((1, LANES, qb), lambda b, i: (0, 0, nxt(b, i)))] + k_specs + [vt_spec] + extra_specs,
        out_specs=pl.BlockSpec((N_PAIRS, qb, LANES), lambda b, i: (0, cur(b, i), 0)),
        out_shape=jax.ShapeDtypeStruct((N_PAIRS, T, LANES), BF16),
        scratch_shapes=[pltpu.VMEM((S, qb), F32), pltpu.VMEM((S, qb), F32),
                        pltpu.VMEM((1, qb), F32), pltpu.VMEM((1, qb), F32)],
        compiler_params=pltpu.CompilerParams(dimension_semantics=("arbitrary", "arbitrary"),
                                             vmem_limit_bytes=VMEM_LIMIT),
        name=name,
    )(qt, qt, k, k, vt, *extra)


def _post_kernel(x_ref, d_ref, g_ref, mem_ref, gm_ref, wkv_ref, wout_ref, gx_ref, wxq_ref, wxo_ref, o_ref, kv_ref,
                 *, tiles_per_seq):
    @pl.when(pl.program_id(0) % tiles_per_seq == 0)
    def _():
        m = _rms(mem_ref[...], gm_ref[...]).astype(BF16)
        kv_ref[...] = jnp.dot(m, wkv_ref[...], preferred_element_type=F32).astype(BF16)

    mix = jnp.concatenate([d_ref[c] for c in range(N_PAIRS)] + [g_ref[c] for c in range(N_PAIRS)], axis=-1)
    x1 = x_ref[...] + jnp.dot(mix, wout_ref[...], preferred_element_type=F32)
    h = _rms(x1, gx_ref[...]).astype(BF16)
    q = (jnp.dot(h, wxq_ref[...], preferred_element_type=F32) * (XATTN_HEAD_DIM ** -0.5 * LOG2E)).astype(BF16)
    outs = []
    for hh in range(XATTN_HEADS):
        sl = slice(hh * XATTN_HEAD_DIM, (hh + 1) * XATTN_HEAD_DIM)
        vsl = slice(D_MODEL + hh * XATTN_HEAD_DIM, D_MODEL + (hh + 1) * XATTN_HEAD_DIM)
        s = lax.dot_general(q[:, sl], kv_ref[:, sl], NT_DIMS, preferred_element_type=F32)
        m = jnp.max(s, axis=-1, keepdims=True)
        p = jnp.exp2(s - m)
        l = jnp.sum(p, axis=-1, keepdims=True)
        o = jnp.dot(p.astype(BF16), kv_ref[:, vsl], preferred_element_type=F32) / l
        outs.append(o.astype(BF16))
    o = jnp.concatenate(outs, axis=-1)
    o_ref[...] = x1 + jnp.dot(o, wxo_ref[...], preferred_element_type=F32)


def _post(x, d_out, g_out, mem, P, S):
    T = x.shape[0]
    tm = TOKEN_TILE
    tiles_per_seq = S // tm
    row = lambda i: (i, 0)
    const = lambda i: (0, 0)
    heads = pl.BlockSpec((N_PAIRS, tm, LANES), lambda i: (0, i, 0))
    return pl.pallas_call(
        functools.partial(_post_kernel, tiles_per_seq=tiles_per_seq),
        grid=(T // tm,),
        in_specs=[pl.BlockSpec((tm, D_MODEL), row), heads, heads,
                  pl.BlockSpec((N_MEM, D_MODEL), lambda i: (i // tiles_per_seq, 0)),
                  pl.BlockSpec((1, D_MODEL), const), pl.BlockSpec((D_MODEL, 2 * D_MODEL), const),
                  pl.BlockSpec((D_MODEL, D_MODEL), const), pl.BlockSpec((1, D_MODEL), const),
                  pl.BlockSpec((D_MODEL, D_MODEL), const), pl.BlockSpec((D_MODEL, D_MODEL), const)],
        out_specs=pl.BlockSpec((tm, D_MODEL), row),
        out_shape=jax.ShapeDtypeStruct((T, D_MODEL), F32),
        scratch_shapes=[pltpu.VMEM((N_MEM, 2 * D_MODEL), BF16)],
        compiler_params=pltpu.CompilerParams(dimension_semantics=("arbitrary",), vmem_limit_bytes=VMEM_LIMIT),
        name="post",
    )(x, d_out, g_out, mem, P["norm_mem_g"], P["w_xkv"], P["w_out"], P["norm_xattn_g"], P["w_xq"], P["w_xo"])


def _ffn_kernel(x_ref, prev_ref, next_ref, gn_ref, wup_ref, cw_ref, cb_ref, wdn_ref, gf_ref, o_ref, act_ref,
                *, tiles_per_seq):
    tm = x_ref.shape[0]
    j = pl.program_id(0) % tiles_per_seq
    has_prev = j > 0
    has_next = j < tiles_per_seq - 1
    x = x_ref[...]
    gn = gn_ref[...]
    rows = tm + 2 * HALO
    h_prev = jnp.where(has_prev, _rms(prev_ref[...], gn), 0.0)
    h_next = jnp.where(has_next, _rms(next_ref[...], gn), 0.0)
    h = jnp.concatenate([h_prev, _rms(x, gn), h_next], axis=0).astype(BF16)

    def conv_cols(start):
        u = jnp.dot(h, wup_ref[:, start:start + FF_CHUNK], preferred_element_type=F32)
        cw = cw_ref[:, start:start + FF_CHUNK]
        y = (pltpu.roll(u, 1, 0) * cw[0:1, :] + u * cw[1:2, :] + pltpu.roll(u, rows - 1, 0) * cw[2:3, :]
             + cb_ref[:, start:start + FF_CHUNK])
        return y[HALO:HALO + tm]

    for c in range(D_FF // FF_CHUNK):
        a = conv_cols(c * FF_CHUNK)
        b = conv_cols(D_FF + c * FF_CHUNK)
        act_ref[:, c * FF_CHUNK:(c + 1) * FF_CHUNK] = (a / (1.0 + jnp.exp(-a)) * b).astype(BF16)

    gf = gf_ref[...]
    for r in range(0, tm, tm // 2):
        rs = slice(r, r + tm // 2)
        x3 = x[rs] + jnp.dot(act_ref[rs, :], wdn_ref[...], preferred_element_type=F32)
        o_ref[rs, :] = _rms(x3, gf)


def _ffn(x, P, S):
    T = x.shape[0]
    tm = TOKEN_TILE
    halo_per_tile = tm // HALO
    last_halo = T // HALO - 1
    row = lambda i: (i, 0)
    const = lambda i: (0, 0)
    return pl.pallas_call(
        functools.partial(_ffn_kernel, tiles_per_seq=S // tm),
        grid=(T // tm,),
        in_specs=[pl.BlockSpec((tm, D_MODEL), row),
                  pl.BlockSpec((HALO, D_MODEL), lambda i: (jnp.maximum(i * halo_per_tile - 1, 0), 0)),
                  pl.BlockSpec((HALO, D_MODEL), lambda i: (jnp.minimum((i + 1) * halo_per_tile, last_halo), 0)),
                  pl.BlockSpec((1, D_MODEL), const),
                  pl.BlockSpec((D_MODEL, 2 * D_FF), const, pipeline_mode=pl.Buffered(1)),
                  pl.BlockSpec((3, 2 * D_FF), const), pl.BlockSpec((1, 2 * D_FF), const),
                  pl.BlockSpec((D_FF, D_MODEL), const, pipeline_mode=pl.Buffered(1)),
                  pl.BlockSpec((1, D_MODEL), const)],
        out_specs=pl.BlockSpec((tm, D_MODEL), row),
        out_shape=jax.ShapeDtypeStruct((T, D_MODEL), F32),
        scratch_shapes=[pltpu.VMEM((tm, D_FF), BF16)],
        compiler_params=pltpu.CompilerParams(dimension_semantics=("arbitrary",), vmem_limit_bytes=VMEM_LIMIT),
        name="ffn",
    )(x, x, x, P["norm_ffn_g"], P["w_up"], P["conv_w"], P["conv_b"], P["w_down"], P["final_norm_g"])


def _angles(p, dim, theta):
    inv = theta ** (-jnp.arange(0, dim, 2, dtype=F32) / dim)
    return p.astype(F32)[:, None] * inv[None, :]


def _rope_tables(S):
    pos = jnp.arange(S, dtype=jnp.int32)
    ang_p = _angles(pos, PARTIAL_ROPE_DIM, ROPE_THETA)
    ang_r = _angles(pos // GRID_W, HEAD_DIM // 2, AXIAL_THETA)
    ang_c = _angles(pos % GRID_W, HEAD_DIM // 2, AXIAL_THETA)
    small = tuple(f(a) for a in (ang_p, ang_r, ang_c) for f in (jnp.cos, jnp.sin))
    cos_p, sin_p, cos_r, sin_r, cos_c, sin_c = small
    lane = jnp.arange(LANES, dtype=jnp.int32) % HEAD_DIM

    half = PARTIAL_ROPE_DIM // 2
    idx = lane % half
    rot = (lane < PARTIAL_ROPE_DIM)[None, :]
    first = ((lane % PARTIAL_ROPE_DIM) < half)[None, :]
    cp = jnp.where(rot, cos_p[:, idx], 1.0)
    sup = jnp.where(rot & first, -sin_p[:, idx], 0.0)
    sdp = jnp.where(rot & ~first, sin_p[:, idx], 0.0)

    half = HEAD_DIM // 4
    idx = lane % half
    by_row = (lane < HEAD_DIM // 2)[None, :]
    first = ((lane % (HEAD_DIM // 2)) < half)[None, :]
    sin_a = jnp.where(by_row, sin_r[:, idx], sin_c[:, idx])
    ca = jnp.where(by_row, cos_r[:, idx], cos_c[:, idx])
    sua = jnp.where(first, -sin_a, 0.0)
    sda = jnp.where(first, 0.0, sin_a)
    return (cp, sup, sdp, ca, sua, sda), tuple(t.T for t in small)


def _trunk(x, mem, P):
    B, S, _ = x.shape
    xf = x.reshape(B * S, D_MODEL)
    dqt, dk, dvt, gqt, gk, gvt = _proj(xf, P, S)
    d_out = _attn(dqt, dk, dvt, P, B, S, diff=True)
    g_out = _attn(gqt, gk, gvt, P, B, S, diff=False)
    x2 = _post(xf, d_out, g_out, mem.reshape(B * N_MEM, D_MODEL), P, S)
    return _ffn(x2, P, S).reshape(B, S, D_MODEL)


def kernel(x_prompt, x_sample, mem_prompt, mem_sample, norm_mix_g, w_in, lambda_q1, lambda_k1, lambda_q2, lambda_k2, diff_subln_g, gqa_q_norm_g, gqa_k_norm_g, w_out, norm_xattn_g, norm_mem_g, w_xq, w_xkv, w_xo, norm_ffn_g, w_up, conv_w, conv_b, w_down, final_norm_g):
    assert w_in.shape[0] == 1, "single-layer trunk"
    s_max = max(x_prompt.shape[1], x_sample.shape[1])
    qb_max = max(ATTN_UNIT_ELEMS // x_prompt.shape[1], ATTN_UNIT_ELEMS // x_sample.shape[1])
    tabs, tabs_t = _rope_tables(s_max)
    w = w_in[0].astype(BF16)
    q0, k0, v0, gq0, gk0, gv0, end = (int(c) for c in np.cumsum((0, DIFF_W, DIFF_W, DIFF_W, GQA_W, KV_W, KV_W)))
    w_in_k = jnp.concatenate([w[:, k0:v0], w[:, gk0:gv0]], axis=1)
    w_in_t = jnp.concatenate([w[:, q0:k0], w[:, v0:gq0], w[:, gq0:gk0], w[:, gv0:end]], axis=1).T
    P = {
        "norm_mix_g": norm_mix_g[0][None, :], "w_in_k": w_in_k, "w_in_t": w_in_t,
        "lambda_q1": lambda_q1, "lambda_k1": lambda_k1, "lambda_q2": lambda_q2, "lambda_k2": lambda_k2,
        "diff_subln_g_t": jnp.broadcast_to(diff_subln_g[0][:, None], (LANES, qb_max)),
        "gqa_q_norm_g_t": jnp.broadcast_to(gqa_q_norm_g[0][:, None], (HEAD_DIM, TOKEN_TILE)),
        "gqa_k_norm_g": jnp.concatenate([gqa_k_norm_g, gqa_k_norm_g], axis=-1),
        "w_out": w_out[0].astype(BF16), "norm_xattn_g": norm_xattn_g, "norm_mem_g": norm_mem_g,
        "w_xq": w_xq[0].astype(BF16), "w_xkv": w_xkv[0].astype(BF16), "w_xo": w_xo[0].astype(BF16),
        "norm_ffn_g": norm_ffn_g, "w_up": w_up[0].astype(BF16), "conv_w": conv_w[0], "conv_b": conv_b,
        "w_down": w_down[0].astype(BF16), "final_norm_g": final_norm_g[None, :],
        "tabs": tabs, "tabs_t": tabs_t,
    }
    y_prompt = _trunk(x_prompt, mem_prompt, P)
    y_sample = _trunk(x_sample, mem_sample, P)
    return (y_prompt, y_sample)
```

```python
import functools
import math

import jax
import jax.numpy as jnp
import numpy as np
from jax import lax
from jax.experimental import pallas as pl
from jax.experimental.pallas import tpu as pltpu

F32 = jnp.float32
BF16 = jnp.bfloat16

D_MODEL = 1024
HEAD_DIM = 64
DIFF_HEADS = 4
GQA_HEADS = 8
GQA_KV_HEADS = 2
DIFF_W = 512
GQA_W = 512
KV_W = GQA_KV_HEADS * HEAD_DIM
ROPE_THETA = 500000.0
PARTIAL_ROPE_DIM = 16
AXIAL_THETA = 10000.0
GRID_W = 64
N_MEM = 256
XATTN_HEADS = 4
XATTN_HEAD_DIM = 256
D_FF = 2816
EPS = 1e-6
LAMBDA_INIT = 0.8 - 0.6 * math.exp(-0.3 * 0)
LOG2E = 1.4426950408889634

LANES = 128
SUBLANES = 8
HALO = SUBLANES
TOKEN_TILE = 1024
FF_CHUNK = 256
ATTN_UNIT_ELEMS = 4096 * 256
ATTN_SCORE_CHUNK_ELEMS = 1024 * 256
ATTN_NARROW_QB = 256
N_PAIRS = 4
VMEM_LIMIT = 56 * 1024 * 1024

NT_DIMS = (((1,), (1,)), ((), ()))


def _rms(x, g):
    ms = jnp.mean(x * x, axis=-1, keepdims=True)
    return x * lax.rsqrt(ms + EPS) * g


def _rope(x, c, s_up, s_dn, shift):
    return x * c + pltpu.roll(x, LANES - shift, 1) * s_up + pltpu.roll(x, shift, 1) * s_dn


def _head_pair_rms(x, g, lo):
    sq = x * x
    tot = jnp.sum(sq, axis=-1, keepdims=True)
    first = jnp.sum(jnp.where(lo, sq, 0.0), axis=-1, keepdims=True)
    ss = jnp.where(lo, first, tot - first)
    return x * lax.rsqrt(ss * (1.0 / HEAD_DIM) + EPS) * g


def _rotate_rows(a, b, cos, sin):
    return a * cos - b * sin, a * sin + b * cos


def _proj_kernel(x_ref, g_ref, wk_ref, wt_ref, cp_ref, sup_ref, sdp_ref, ca_ref, sua_ref, sda_ref, gkn_ref,
                 cpt_ref, spt_ref, crt_ref, srt_ref, cct_ref, sct_ref, gqn_ref,
                 dqt_ref, dk_ref, dvt_ref, gqt_ref, gk_ref, gvt_ref):
    tm = x_ref.shape[0]
    h = _rms(x_ref[...], g_ref[...]).astype(BF16)
    qscale = HEAD_DIM ** -0.5 * LOG2E

    proj_k = jnp.dot(h, wk_ref[...], preferred_element_type=F32)
    proj_t = lax.dot_general(wt_ref[...], h, NT_DIMS, preferred_element_type=F32)

    def rows_t(start, n):
        return proj_t[start:start + n]

    cpt, spt = cpt_ref[...], spt_ref[...]
    half = PARTIAL_ROPE_DIM // 2
    for c in range(DIFF_HEADS):
        q = rows_t(c * LANES, LANES)
        parts = []
        for m in range(2):
            b0 = m * HEAD_DIM
            ra, rb = _rotate_rows(q[b0:b0 + half], q[b0 + half:b0 + 2 * half], cpt, spt)
            parts += [ra, rb, q[b0 + 2 * half:b0 + HEAD_DIM]]
        dqt_ref[c] = (jnp.concatenate(parts, axis=0) * qscale).astype(BF16)
        dvt_ref[c] = rows_t(DIFF_W + c * LANES, LANES).astype(BF16)

    crt, srt, cct, sct = crt_ref[...], srt_ref[...], cct_ref[...], sct_ref[...]
    gqn = gqn_ref[...]
    quarter = HEAD_DIM // 4
    for c in range(GQA_HEADS // 2):
        q = rows_t(2 * DIFF_W + c * LANES, LANES)
        parts = []
        for m in range(2):
            x = q[m * HEAD_DIM:(m + 1) * HEAD_DIM]
            ss = jnp.sum(x * x, axis=0, keepdims=True)
            x = x * lax.rsqrt(ss * (1.0 / HEAD_DIM) + EPS) * gqn
            parts += _rotate_rows(x[0:quarter], x[quarter:2 * quarter], crt, srt)
            parts += _rotate_rows(x[2 * quarter:3 * quarter], x[3 * quarter:HEAD_DIM], cct, sct)
        gqt_ref[c] = (jnp.concatenate(parts, axis=0) * qscale).astype(BF16)
    gvt_ref[...] = rows_t(2 * DIFF_W + GQA_W, KV_W).astype(BF16)

    lo = lax.broadcasted_iota(jnp.int32, (tm, LANES), 1) < HEAD_DIM
    cp, sup, sdp = cp_ref[...], sup_ref[...], sdp_ref[...]
    for c in range(DIFF_HEADS):
        k = proj_k[:, c * LANES:(c + 1) * LANES]
        dk_ref[c] = _rope(k, cp, sup, sdp, half).astype(BF16)
    k = _head_pair_rms(proj_k[:, DIFF_W:DIFF_W + KV_W], gkn_ref[...], lo)
    gk_ref[...] = _rope(k, ca_ref[...], sua_ref[...], sda_ref[...], quarter).astype(BF16)


def _proj(x, P, S):
    T = x.shape[0]
    tm = TOKEN_TILE
    tiles_per_seq = S // tm
    row = lambda i: (i, 0)
    const = lambda i: (0, 0)
    tab_spec = pl.BlockSpec((tm, LANES), lambda i: (i % tiles_per_seq, 0))
    tab_t = lambda n: pl.BlockSpec((n, tm), lambda i: (0, i % tiles_per_seq))
    vec = lambda n: pl.BlockSpec((1, n), const)
    head_t = lambda n, r: pl.BlockSpec((n, r, tm), lambda i: (0, 0, i))
    wk, wt = P["w_in_k"], P["w_in_t"]
    return pl.pallas_call(
        _proj_kernel,
        grid=(T // tm,),
        in_specs=[pl.BlockSpec((tm, D_MODEL), row), vec(D_MODEL),
                  pl.BlockSpec(wk.shape, const), pl.BlockSpec(wt.shape, const)]
                 + [tab_spec] * 6 + [vec(LANES)]
                 + [tab_t(PARTIAL_ROPE_DIM // 2)] * 2 + [tab_t(HEAD_DIM // 4)] * 4
                 + [pl.BlockSpec((HEAD_DIM, tm), const)],
        out_specs=[head_t(DIFF_HEADS, LANES), pl.BlockSpec((DIFF_HEADS, tm, LANES), lambda i: (0, i, 0)),
                   head_t(DIFF_HEADS, LANES), head_t(GQA_HEADS // 2, LANES),
                   pl.BlockSpec((tm, LANES), row), pl.BlockSpec((KV_W, tm), lambda i: (0, i))],
        out_shape=[jax.ShapeDtypeStruct((DIFF_HEADS, LANES, T), BF16),
                   jax.ShapeDtypeStruct((DIFF_HEADS, T, LANES), BF16),
                   jax.ShapeDtypeStruct((DIFF_HEADS, LANES, T), BF16),
                   jax.ShapeDtypeStruct((GQA_HEADS // 2, LANES, T), BF16),
                   jax.ShapeDtypeStruct((T, LANES), BF16),
                   jax.ShapeDtypeStruct((KV_W, T), BF16)],
        compiler_params=pltpu.CompilerParams(dimension_semantics=("arbitrary",), vmem_limit_bytes=VMEM_LIMIT),
        name="proj",
    )(x, P["norm_mix_g"], wk, wt, *P["tabs"], P["gqa_k_norm_g"], *P["tabs_t"], P["gqa_q_norm_g_t"])


def _pad_queries(q_rows, hi):
    z = jnp.zeros_like(q_rows)
    return jnp.concatenate([z, q_rows] if hi else [q_rows, z], axis=0)


def _column_max(chunks):
    while len(chunks) > 1:
        chunks = [jnp.maximum(a, b) for a, b in zip(chunks[0::2], chunks[1::2])]
    return jnp.max(chunks[0], axis=0, keepdims=True)


def _score_chunk(k_rows, qpad, s_ref, c, rows, acc):
    s = jnp.dot(k_rows(c, rows), qpad, preferred_element_type=F32)
    s_ref[c:c + rows, :] = s
    half = rows // 2
    for r in (0, half):
        acc.append(jnp.max(s[r:r + half].reshape(half // SUBLANES, SUBLANES, s.shape[1]), axis=0))


def _score_unit(q_rows, hi, k_rows, s_ref, m_ref):
    n, qb = s_ref.shape
    rows = ATTN_SCORE_CHUNK_ELEMS // qb
    qpad = _pad_queries(q_rows, hi)
    acc = []
    for c in range(0, n, rows):
        _score_chunk(k_rows, qpad, s_ref, c, rows, acc)
    m_ref[...] = _column_max(acc)


def _phase(q_rows, hi, k_rows, s_nxt, m_nxt, s_cur, m_cur, vt_cols):
    n, qb = s_cur.shape
    a_rows = ATTN_SCORE_CHUNK_ELEMS // qb
    b_rows = a_rows // 2
    interleave = qb > ATTN_NARROW_QB
    qpad = _pad_queries(q_rows, hi)
    m = m_cur[...]
    acc, sums, pending = [], [], []
    o = None

    def value_matmuls(o):
        for c0, p in pending:
            oc = jnp.dot(vt_cols(c0, b_rows), p, preferred_element_type=F32)
            o = oc if o is None else o + oc
        return o

    def softmax_chunk(r):
        p = jnp.exp2(s_cur[r:r + b_rows, :] - m)
        sums.append(jnp.sum(p.reshape(b_rows // SUBLANES, SUBLANES, qb), axis=0))
        return r, p.astype(BF16)

    for c in range(0, n, a_rows):
        _score_chunk(k_rows, qpad, s_nxt, c, a_rows, acc)
        if interleave:
            o = value_matmuls(o)
            pending = [softmax_chunk(r) for r in (c, c + b_rows)]
    if not interleave:
        for r in range(0, n, b_rows):
            pending = [softmax_chunk(r)]
            o = value_matmuls(o)
    else:
        o = value_matmuls(o)
    m_nxt[...] = _column_max(acc)
    while len(sums) > 1:
        sums = [a + b for a, b in zip(sums[0::2], sums[1::2])]
    return o / jnp.sum(sums[0], axis=0, keepdims=True)


def _attn_kernel(*refs, diff):
    if diff:
        (qt_ref, qtn_ref, k_ref, kn_ref, vt_ref, lq1_ref, lk1_ref, lq2_ref, lk2_ref, sg_ref,
         o_ref, s0, s1, m0, m1) = refs
        lam = (jnp.exp(jnp.sum(lq1_ref[...] * lk1_ref[...], axis=-1, keepdims=True))
               - jnp.exp(jnp.sum(lq2_ref[...] * lk2_ref[...], axis=-1, keepdims=True)) + LAMBDA_INIT)
    else:
        qt_ref, qtn_ref, k_ref, kn_ref, vt_ref, o_ref, s0, s1, m0, m1 = refs

    def keys(j):
        return (lambda c, r: k_ref[j, c:c + r, :]) if diff else (lambda c, r: k_ref[c:c + r, :])

    def next_keys(c, r):
        return kn_ref[0, c:c + r, :] if diff else kn_ref[c:c + r, :]

    def values(j):
        head = j if diff else j // 2
        return lambda c, r: vt_ref[head, :, c:c + r]

    def his(j):
        return (False, True) if diff else (j // 2 == 1, j // 2 == 1)

    @pl.when(jnp.logical_and(pl.program_id(0) == 0, pl.program_id(1) == 0))
    def _():
        _score_unit(qt_ref[0, 0:HEAD_DIM, :], False, keys(0), s0, m0)

    for j in range(N_PAIRS):
        oa = _phase(qt_ref[j, HEAD_DIM:2 * HEAD_DIM, :], his(j)[1], keys(j), s1, m1, s0, m0, values(j))
        if j + 1 < N_PAIRS:
            nxt = (qt_ref[j + 1, 0:HEAD_DIM, :], his(j + 1)[0], keys(j + 1))
        else:
            nxt = (qtn_ref[0, 0:HEAD_DIM, :], False, next_keys)
        ob = _phase(*nxt, s0, m0, s1, m1, values(j))
        if diff:
            d = oa - lam * ob
            ms = jnp.mean(d * d, axis=0, keepdims=True)
            out = d * lax.rsqrt(ms + EPS) * sg_ref[...] * (1.0 - LAMBDA_INIT)
        else:
            out = jnp.concatenate([oa, ob], axis=0)
        o_ref[j] = out.T.astype(BF16)


def _attn(qt, k, vt, P, B, S, diff):
    T = qt.shape[-1]
    qb = ATTN_UNIT_ELEMS // S
    nq = S // qb
    last = B * nq - 1
    cur = lambda b, i: b * nq + i
    nxt = lambda b, i: jnp.minimum(b * nq + i + 1, last)
    const = lambda b, i: (0, 0)
    if diff:
        k_specs = [pl.BlockSpec((DIFF_HEADS, S, LANES), lambda b, i: (0, b, 0)),
                   pl.BlockSpec((1, S, LANES), lambda b, i: (0, nxt(b, i) // nq, 0))]
        vt_spec = pl.BlockSpec((DIFF_HEADS, LANES, S), lambda b, i: (0, 0, b))
        extra = [P["lambda_q1"], P["lambda_k1"], P["lambda_q2"], P["lambda_k2"], P["diff_subln_g_t"]]
        extra_specs = [pl.BlockSpec((1, HEAD_DIM), const)] * 4 + [pl.BlockSpec((LANES, qb), const)]
        name = "diff_attn"
    else:
        k_specs = [pl.BlockSpec((S, LANES), lambda b, i: (b, 0)),
                   pl.BlockSpec((S, LANES), lambda b, i: (nxt(b, i) // nq, 0))]
        vt = vt.reshape(GQA_KV_HEADS, HEAD_DIM, T)
        vt_spec = pl.BlockSpec((GQA_KV_HEADS, HEAD_DIM, S), lambda b, i: (0, 0, b))
        extra, extra_specs = [], []
        name = "gqa_attn"
    return pl.pallas_call(
        functools.partial(_attn_kernel, diff=diff),
        grid=(B, nq),
        in_specs=[pl.BlockSpec((N_PAIRS, LANES, qb), lambda b, i: (0, 0, cur(b, i))),
                  pl.BlockSpec((1, LANES, qb), lambda b, i: (0, 0, nxt(b, i)))] + k_specs + [vt_spec] + extra_specs,
        out_specs=pl.BlockSpec((N_PAIRS, qb, LANES), lambda b, i: (0, cur(b, i), 0)),
        out_shape=jax.ShapeDtypeStruct((N_PAIRS, T, LANES), BF16),
        scratch_shapes=[pltpu.VMEM((S, qb), F32), pltpu.VMEM((S, qb), F32),
                        pltpu.VMEM((1, qb), F32), pltpu.VMEM((1, qb), F32)],
        compiler_params=pltpu.CompilerParams(dimension_semantics=("arbitrary", "arbitrary"),
                                             vmem_limit_bytes=VMEM_LIMIT),
        name=name,
    )(qt, qt, k, k, vt, *extra)


def _post_kernel(x_ref, d_ref, g_ref, mem_ref, gm_ref, wkv_ref, wout_ref, gx_ref, wxq_ref, wxo_ref, o_ref, kv_ref,
                 *, tiles_per_seq):
    @pl.when(pl.program_id(0) % tiles_per_seq == 0)
    def _():
        m = _rms(mem_ref[...], gm_ref[...]).astype(BF16)
        kv_ref[...] = jnp.dot(m, wkv_ref[...], preferred_element_type=F32).astype(BF16)

    mix = jnp.concatenate([d_ref[c] for c in range(N_PAIRS)] + [g_ref[c] for c in range(N_PAIRS)], axis=-1)
    x1 = x_ref[...] + jnp.dot(mix, wout_ref[...], preferred_element_type=F32)
    h = _rms(x1, gx_ref[...]).astype(BF16)
    q = (jnp.dot(h, wxq_ref[...], preferred_element_type=F32) * (XATTN_HEAD_DIM ** -0.5 * LOG2E)).astype(BF16)
    outs = []
    for hh in range(XATTN_HEADS):
        sl = slice(hh * XATTN_HEAD_DIM, (hh + 1) * XATTN_HEAD_DIM)
        vsl = slice(D_MODEL + hh * XATTN_HEAD_DIM, D_MODEL + (hh + 1) * XATTN_HEAD_DIM)
        s = lax.dot_general(q[:, sl], kv_ref[:, sl], NT_DIMS, preferred_element_type=F32)
        m = jnp.max(s, axis=-1, keepdims=True)
        p = jnp.exp2(s - m)
        l = jnp.sum(p, axis=-1, keepdims=True)
        o = jnp.dot(p.astype(BF16), kv_ref[:, vsl], preferred_element_type=F32) / l
        outs.append(o.astype(BF16))
    o = jnp.concatenate(outs, axis=-1)
    o_ref[...] = x1 + jnp.dot(o, wxo_ref[...], preferred_element_type=F32)


def _post(x, d_out, g_out, mem, P, S):
    T = x.shape[0]
    tm = TOKEN_TILE
    tiles_per_seq = S // tm
    row = lambda i: (i, 0)
    const = lambda i: (0, 0)
    heads = pl.BlockSpec((N_PAIRS, tm, LANES), lambda i: (0, i, 0))
    return pl.pallas_call(
        functools.partial(_post_kernel, tiles_per_seq=tiles_per_seq),
        grid=(T // tm,),
        in_specs=[pl.BlockSpec((tm, D_MODEL), row), heads, heads,
                  pl.BlockSpec((N_MEM, D_MODEL), lambda i: (i // tiles_per_seq, 0)),
                  pl.BlockSpec((1, D_MODEL), const), pl.BlockSpec((D_MODEL, 2 * D_MODEL), const),
                  pl.BlockSpec((D_MODEL, D_MODEL), const), pl.BlockSpec((1, D_MODEL), const),
                  pl.BlockSpec((D_MODEL, D_MODEL), const), pl.BlockSpec((D_MODEL, D_MODEL), const)],
        out_specs=pl.BlockSpec((tm, D_MODEL), row),
        out_shape=jax.ShapeDtypeStruct((T, D_MODEL), F32),
        scratch_shapes=[pltpu.VMEM((N_MEM, 2 * D_MODEL), BF16)],
        compiler_params=pltpu.CompilerParams(dimension_semantics=("arbitrary",), vmem_limit_bytes=VMEM_LIMIT),
        name="post",
    )(x, d_out, g_out, mem, P["norm_mem_g"], P["w_xkv"], P["w_out"], P["norm_xattn_g"], P["w_xq"], P["w_xo"])


def _ffn_kernel(x_ref, prev_ref, next_ref, gn_ref, wup_ref, cw_ref, cb_ref, wdn_ref, gf_ref, o_ref, act_ref,
                *, tiles_per_seq):
    tm = x_ref.shape[0]
    j = pl.program_id(0) % tiles_per_seq
    has_prev = j > 0
    has_next = j < tiles_per_seq - 1
    x = x_ref[...]
    gn = gn_ref[...]
    rows = tm + 2 * HALO
    h_prev = jnp.where(has_prev, _rms(prev_ref[...], gn), 0.0)
    h_next = jnp.where(has_next, _rms(next_ref[...], gn), 0.0)
    h = jnp.concatenate([h_prev, _rms(x, gn), h_next], axis=0).astype(BF16)

    def conv_cols(start):
        u = jnp.dot(h, wup_ref[:, start:start + FF_CHUNK], preferred_element_type=F32)
        cw = cw_ref[:, start:start + FF_CHUNK]
        y = (pltpu.roll(u, 1, 0) * cw[0:1, :] + u * cw[1:2, :] + pltpu.roll(u, rows - 1, 0) * cw[2:3, :]
             + cb_ref[:, start:start + FF_CHUNK])
        return y[HALO:HALO + tm]

    for c in range(D_FF // FF_CHUNK):
        a = conv_cols(c * FF_CHUNK)
        b = conv_cols(D_FF + c * FF_CHUNK)
        act_ref[:, c * FF_CHUNK:(c + 1) * FF_CHUNK] = (a / (1.0 + jnp.exp(-a)) * b).astype(BF16)

    gf = gf_ref[...]
    for r in range(0, tm, tm // 2):
        rs = slice(r, r + tm // 2)
        x3 = x[rs] + jnp.dot(act_ref[rs, :], wdn_ref[...], preferred_element_type=F32)
        o_ref[rs, :] = _rms(x3, gf)


def _ffn(x, P, S):
    T = x.shape[0]
    tm = TOKEN_TILE
    halo_per_tile = tm // HALO
    last_halo = T // HALO - 1
    row = lambda i: (i, 0)
    const = lambda i: (0, 0)
    return pl.pallas_call(
        functools.partial(_ffn_kernel, tiles_per_seq=S // tm),
        grid=(T // tm,),
        in_specs=[pl.BlockSpec((tm, D_MODEL), row),
                  pl.BlockSpec((HALO, D_MODEL), lambda i: (jnp.maximum(i * halo_per_tile - 1, 0), 0)),
                  pl.BlockSpec((HALO, D_MODEL), lambda i: (jnp.minimum((i + 1) * halo_per_tile, last_halo), 0)),
                  pl.BlockSpec((1, D_MODEL), const),
                  pl.BlockSpec((D_MODEL, 2 * D_FF), const, pipeline_mode=pl.Buffered(1)),
                  pl.BlockSpec((3, 2 * D_FF), const), pl.BlockSpec((1, 2 * D_FF), const),
                  pl.BlockSpec((D_FF, D_MODEL), const, pipeline_mode=pl.Buffered(1)),
                  pl.BlockSpec((1, D_MODEL), const)],
        out_specs=pl.BlockSpec((tm, D_MODEL), row),
        out_shape=jax.ShapeDtypeStruct((T, D_MODEL), F32),
        scratch_shapes=[pltpu.VMEM((tm, D_FF), BF16)],
        compiler_params=pltpu.CompilerParams(dimension_semantics=("arbitrary",), vmem_limit_bytes=VMEM_LIMIT),
        name="ffn",
    )(x, x, x, P["norm_ffn_g"], P["w_up"], P["conv_w"], P["conv_b"], P["w_down"], P["final_norm_g"])


def _angles(p, dim, theta):
    inv = theta ** (-jnp.arange(0, dim, 2, dtype=F32) / dim)
    return p.astype(F32)[:, None] * inv[None, :]


def _rope_tables(S):
    pos = jnp.arange(S, dtype=jnp.int32)
    ang_p = _angles(pos, PARTIAL_ROPE_DIM, ROPE_THETA)
    ang_r = _angles(pos // GRID_W, HEAD_DIM // 2, AXIAL_THETA)
    ang_c = _angles(pos % GRID_W, HEAD_DIM // 2, AXIAL_THETA)
    small = tuple(f(a) for a in (ang_p, ang_r, ang_c) for f in (jnp.cos, jnp.sin))
    cos_p, sin_p, cos_r, sin_r, cos_c, sin_c = small
    lane = jnp.arange(LANES, dtype=jnp.int32) % HEAD_DIM

    half = PARTIAL_ROPE_DIM // 2
    idx = lane % half
    rot = (lane < PARTIAL_ROPE_DIM)[None, :]
    first = ((lane % PARTIAL_ROPE_DIM) < half)[None, :]
    cp = jnp.where(rot, cos_p[:, idx], 1.0)
    sup = jnp.where(rot & first, -sin_p[:, idx], 0.0)
    sdp = jnp.where(rot & ~first, sin_p[:, idx], 0.0)

    half = HEAD_DIM // 4
    idx = lane % half
    by_row = (lane < HEAD_DIM // 2)[None, :]
    first = ((lane % (HEAD_DIM // 2)) < half)[None, :]
    sin_a = jnp.where(by_row, sin_r[:, idx], sin_c[:, idx])
    ca = jnp.where(by_row, cos_r[:, idx], cos_c[:, idx])
    sua = jnp.where(first, -sin_a, 0.0)
    sda = jnp.where(first, 0.0, sin_a)
    return (cp, sup, sdp, ca, sua, sda), tuple(t.T for t in small)


def _trunk(x, mem, P):
    B, S, _ = x.shape
    xf = x.reshape(B * S, D_MODEL)
    dqt, dk, dvt, gqt, gk, gvt = _proj(xf, P, S)
    d_out = _attn(dqt, dk, dvt, P, B, S, diff=True)
    g_out = _attn(gqt, gk, gvt, P, B, S, diff=False)
    x2 = _post(xf, d_out, g_out, mem.reshape(B * N_MEM, D_MODEL), P, S)
    return _ffn(x2, P, S).reshape(B, S, D_MODEL)


def kernel(x_prompt, x_sample, mem_prompt, mem_sample, norm_mix_g, w_in, lambda_q1, lambda_k1, lambda_q2, lambda_k2, diff_subln_g, gqa_q_norm_g, gqa_k_norm_g, w_out, norm_xattn_g, norm_mem_g, w_xq, w_xkv, w_xo, norm_ffn_g, w_up, conv_w, conv_b, w_down, final_norm_g):
    assert w_in.shape[0] == 1, "single-layer trunk"
    s_max = max(x_prompt.shape[1], x_sample.shape[1])
    qb_max = max(ATTN_UNIT_ELEMS // x_prompt.shape[1], ATTN_UNIT_ELEMS // x_sample.shape[1])
    tabs, tabs_t = _rope_tables(s_max)
    w = w_in[0].astype(BF16)
    q0, k0, v0, gq0, gk0, gv0, end = (int(c) for c in np.cumsum((0, DIFF_W, DIFF_W, DIFF_W, GQA_W, KV_W, KV_W)))
    w_in_k = jnp.concatenate([w[:, k0:v0], w[:, gk0:gv0]], axis=1)
    w_in_t = jnp.concatenate([w[:, q0:k0], w[:, v0:gq0], w[:, gq0:gk0], w[:, gv0:end]], axis=1).T
    P = {
        "norm_mix_g": norm_mix_g[0][None, :], "w_in_k": w_in_k, "w_in_t": w_in_t,
        "lambda_q1": lambda_q1, "lambda_k1": lambda_k1, "lambda_q2": lambda_q2, "lambda_k2": lambda_k2,
        "diff_subln_g_t": jnp.broadcast_to(diff_subln_g[0][:, None], (LANES, qb_max)),
        "gqa_q_norm_g_t": jnp.broadcast_to(gqa_q_norm_g[0][:, None], (HEAD_DIM, TOKEN_TILE)),
        "gqa_k_norm_g": jnp.concatenate([gqa_k_norm_g, gqa_k_norm_g], axis=-1),
        "w_out": w_out[0].astype(BF16), "norm_xattn_g": norm_xattn_g, "norm_mem_g": norm_mem_g,
        "w_xq": w_xq[0].astype(BF16), "w_xkv": w_xkv[0].astype(BF16), "w_xo": w_xo[0].astype(BF16),
        "norm_ffn_g": norm_ffn_g, "w_up": w_up[0].astype(BF16), "conv_w": conv_w[0], "conv_b": conv_b,
        "w_down": w_down[0].astype(BF16), "final_norm_g": final_norm_g[None, :],
        "tabs": tabs, "tabs_t": tabs_t,
    }
    y_prompt = _trunk(x_prompt, mem_prompt, P)
    y_sample = _trunk(x_sample, mem_sample, P)
    return (y_prompt, y_sample)
```
